```python
import math
import jax, jax.numpy as jnp
from jax import lax
import numpy as np

D_MODEL = 1024
BATCH = 4
SEQ = 8192
DEPTH = 2

D_SSD = D_MODEL
SSD_HEADDIM = 64
SSD_HEADS = D_SSD // SSD_HEADDIM
SSD_GROUPS = 4
SSD_HPG = SSD_HEADS // SSD_GROUPS
D_STATE = 128
SSD_CONV = 4
CHUNK = 128
D_FOX = D_MODEL
FOX_HEADDIM = 64
FOX_HEADS = D_FOX // FOX_HEADDIM
Q_BLOCK = 128
D_MIX = D_SSD + D_FOX
EVEN_SIZES = (D_MIX, D_SSD + 2 * SSD_GROUPS * D_STATE, SSD_HEADS, D_FOX, D_FOX, D_FOX, FOX_HEADS)
EVEN_IN = D_MIX + D_SSD + 2 * SSD_GROUPS * D_STATE + SSD_HEADS + 3 * D_FOX + FOX_HEADS
D_CONV = 2 * D_MODEL
CONV_WIDTH = 31
ODD_IN = 3 * D_CONV
N_EVEN = (DEPTH + 1) // 2
N_ODD = DEPTH // 2
EPS = 1e-6

kernel_name = "hybrid_ssd_fox_conformer_trunk"


def _split_points(sizes):
    pts, acc = [], 0
    for s in sizes[:-1]:
        acc += s
        pts.append(acc)
    return pts


def rmsnorm(x, g):
    xf = x.astype(jnp.float32)
    y = xf * lax.rsqrt(jnp.mean(xf * xf, axis=-1, keepdims=True) + EPS)
    return (y * g.astype(jnp.float32)).astype(x.dtype)


def layernorm(x, g, b):
    xf = x.astype(jnp.float32)
    mu = jnp.mean(xf, axis=-1, keepdims=True)
    xc = xf - mu
    y = xc * lax.rsqrt(jnp.mean(xc * xc, axis=-1, keepdims=True) + EPS)
    return (y * g.astype(jnp.float32) + b.astype(jnp.float32)).astype(x.dtype)


def causal_depthwise_conv(x, w, b):
    k, c = w.shape
    y = lax.conv_general_dilated(x, w[:, None, :], window_strides=(1,), padding=[(k - 1, 0)],
                                 dimension_numbers=('NWC', 'WIO', 'NWC'), feature_group_count=c)
    return y + b


def ssd_chunked(x, dt, a, bm, cm):
    bsz, s, g, r, p = x.shape
    n = bm.shape[-1]
    nc = s // CHUNK
    xd = (x * dt[..., None]).reshape(bsz, nc, CHUNK, g, r, p)
    da = (dt * a).reshape(bsz, nc, CHUNK, g, r)
    bc = bm.reshape(bsz, nc, CHUNK, g, n)
    cc = cm.reshape(bsz, nc, CHUNK, g, n)
    cs = jnp.cumsum(da, axis=2)
    li = jnp.arange(CHUNK)
    causal = (li[:, None] >= li[None, :])[None, None, :, :, None, None]
    seg = cs[:, :, :, None] - cs[:, :, None, :]
    decay = jnp.exp(jnp.where(causal, seg, -jnp.inf))
    cb = jnp.einsum('bclgn,bcsgn->bclsg', cc, bc)
    y_diag = jnp.einsum('bclsgr,bcsgrp->bclgrp', cb[..., None] * decay, xd)
    decay_to_end = jnp.exp(cs[:, :, -1:] - cs)
    chunk_states = jnp.einsum('bclgn,bclgrp->bcgrpn', bc, xd * decay_to_end[..., None])
    chunk_decay = jnp.exp(cs[:, :, -1])

    def step(h, inp):
        st, dec = inp
        h_new = h * dec[..., None, None] + st
        return h_new, h

    h0 = jnp.zeros((bsz, g, r, p, n), chunk_states.dtype)
    _, h_in = lax.scan(step, h0, (jnp.moveaxis(chunk_states, 1, 0), jnp.moveaxis(chunk_decay, 1, 0)))
    h_in = jnp.moveaxis(h_in, 0, 1)
    y_off = jnp.einsum('bclgn,bcgrpn->bclgrp', cc, h_in) * jnp.exp(cs)[..., None]
    return (y_diag + y_off).reshape(bsz, s, g, r, p)


def forgetting_attention(q, k, v, log_f):
    bsz, s, h, p = q.shape
    c = jnp.cumsum(log_f, axis=1)
    ck = jnp.transpose(c, (0, 2, 1))
    nb = s // Q_BLOCK
    qb = jnp.moveaxis(q.reshape(bsz, nb, Q_BLOCK, h, p), 1, 0)
    cqb = jnp.moveaxis(c.reshape(bsz, nb, Q_BLOCK, h), 1, 0)
    kpos = jnp.arange(s)
    scale = p ** -0.5

    def block(args):
        qi, cqi, i = args
        qpos = i * Q_BLOCK + jnp.arange(Q_BLOCK)
        logits = jnp.einsum('bqhp,bkhp->bhqk', qi, k).astype(jnp.float32) * scale
        logits = logits + jnp.transpose(cqi, (0, 2, 1))[..., None] - ck[:, :, None, :]
        logits = jnp.where(kpos[None, :] <= qpos[:, None], logits, -jnp.inf)
        w = jax.nn.softmax(logits, axis=-1).astype(v.dtype)
        return jnp.einsum('bhqk,bkhp->bqhp', w, v)

    out = lax.map(block, (qb, cqb, jnp.arange(nb)))
    return jnp.moveaxis(out, 0, 1).reshape(bsz, s, h, p)


def ssd_fox_layer(x, g_pre, w_in, conv_w, conv_b, dt_bias, a_log, d_skip, fgate_b, ssd_norm, w_out, g_post):
    bsz, s, _ = x.shape
    u = rmsnorm(x, g_pre)
    proj = jnp.einsum('bsd,de->bse', u, w_in)
    z, xbc, dt_raw, q, k, v, f_raw = jnp.split(proj, _split_points(EVEN_SIZES), axis=-1)
    z_ssd, z_fox = jnp.split(z, [D_SSD], axis=-1)
    xbc = jax.nn.silu(causal_depthwise_conv(xbc, conv_w, conv_b))
    xs, bm, cm = jnp.split(xbc, [D_SSD, D_SSD + SSD_GROUPS * D_STATE], axis=-1)
    dt = jax.nn.softplus(dt_raw + dt_bias).reshape(bsz, s, SSD_GROUPS, SSD_HPG)
    a = (-jnp.exp(a_log)).reshape(SSD_GROUPS, SSD_HPG)
    xs = xs.reshape(bsz, s, SSD_GROUPS, SSD_HPG, SSD_HEADDIM)
    y = ssd_chunked(xs, dt, a, bm.reshape(bsz, s, SSD_GROUPS, D_STATE), cm.reshape(bsz, s, SSD_GROUPS, D_STATE))
    y = (y + xs * d_skip.reshape(SSD_GROUPS, SSD_HPG)[:, :, None]).reshape(bsz, s, D_SSD)
    yg = (y * jax.nn.silu(z_ssd)).reshape(bsz, s, SSD_GROUPS, D_SSD // SSD_GROUPS).astype(jnp.float32)
    yg = yg * lax.rsqrt(jnp.mean(yg * yg, axis=-1, keepdims=True) + EPS)
    y = (yg.reshape(bsz, s, D_SSD) * ssd_norm.astype(jnp.float32)).astype(x.dtype)
    log_f = jax.nn.log_sigmoid((f_raw + fgate_b).astype(jnp.float32))
    o = forgetting_attention(q.reshape(bsz, s, FOX_HEADS, FOX_HEADDIM),
                             k.reshape(bsz, s, FOX_HEADS, FOX_HEADDIM),
                             v.reshape(bsz, s, FOX_HEADS, FOX_HEADDIM), log_f)
    o = o.reshape(bsz, s, D_FOX) * jax.nn.silu(z_fox)
    out = jnp.einsum('bse,ed->bsd', jnp.concatenate([y, o], axis=-1), w_out)
    return x + rmsnorm(out, g_post)


def conformer_conv_layer(x, g_pre, w_in, conv_w, conv_b, ln_g, ln_b, w_out, g_post):
    u = rmsnorm(x, g_pre)
    proj = jnp.einsum('bsd,de->bse', u, w_in)
    val, glu_gate, z = jnp.split(proj, [D_CONV, 2 * D_CONV], axis=-1)
    h = val * jax.nn.sigmoid(glu_gate)
    h = causal_depthwise_conv(h, conv_w, conv_b)
    h = jax.nn.silu(layernorm(h, ln_g, ln_b))
    h = h * jax.nn.silu(z)
    out = jnp.einsum('bse,ed->bsd', h, w_out)
    return x + rmsnorm(out, g_post)


def setup_inputs(seed: int = 0) -> dict:
    key = jax.random.key(seed)
    ks = jax.random.split(key, 20)
    f32 = jnp.float32

    def nrm(k, shape, scale):
        return jax.random.normal(k, shape, f32) * scale

    def gain(k, shape):
        return 1.0 + 0.02 * jax.random.normal(k, shape, f32)

    ne, no = N_EVEN, N_ODD
    x = jax.random.normal(ks[0], (BATCH, SEQ, D_MODEL), f32)
    dt0 = jnp.exp(jax.random.uniform(ks[5], (ne, SSD_HEADS), f32, minval=math.log(1e-3), maxval=math.log(1e-1)))
    e_dt_bias = dt0 + jnp.log(-jnp.expm1(-dt0))
    e_a_log = jnp.log(jax.random.uniform(ks[6], (ne, SSD_HEADS), f32, minval=1.0, maxval=16.0))
    return {
        "x": x,
        "e_norm_pre": gain(ks[1], (ne, D_MODEL)),
        "e_w_in": nrm(ks[2], (ne, D_MODEL, EVEN_IN), D_MODEL ** -0.5),
        "e_conv_w": nrm(ks[3], (ne, SSD_CONV, D_SSD + 2 * SSD_GROUPS * D_STATE), SSD_CONV ** -0.5),
        "e_conv_b": nrm(ks[4], (ne, D_SSD + 2 * SSD_GROUPS * D_STATE), 0.02),
        "e_dt_bias": e_dt_bias,
        "e_a_log": e_a_log,
        "e_d_skip": 1.0 + 0.1 * jax.random.normal(ks[7], (ne, SSD_HEADS), f32),
        "e_fgate_b": jax.random.uniform(ks[8], (ne, FOX_HEADS), f32, minval=1.0, maxval=6.0),
        "e_ssd_norm": gain(ks[9], (ne, D_SSD)),
        "e_w_out": nrm(ks[10], (ne, D_MIX, D_MODEL), D_MIX ** -0.5),
        "e_norm_post": gain(ks[11], (ne, D_MODEL)),
        "o_norm_pre": gain(ks[12], (no, D_MODEL)),
        "o_w_in": nrm(ks[13], (no, D_MODEL, ODD_IN), D_MODEL ** -0.5),
        "o_conv_w": nrm(ks[14], (no, CONV_WIDTH, D_CONV), CONV_WIDTH ** -0.5),
        "o_conv_b": nrm(ks[15], (no, D_CONV), 0.02),
        "o_ln_g": gain(ks[16], (no, D_CONV)),
        "o_ln_b": nrm(ks[17], (no, D_CONV), 0.02),
        "o_w_out": nrm(ks[18], (no, D_CONV, D_MODEL), D_CONV ** -0.5),
        "o_norm_post": gain(ks[19], (no, D_MODEL)),
    }


def reference(x, e_norm_pre, e_w_in, e_conv_w, e_conv_b, e_dt_bias, e_a_log, e_d_skip, e_fgate_b,
              e_ssd_norm, e_w_out, e_norm_post, o_norm_pre, o_w_in, o_conv_w, o_conv_b, o_ln_g, o_ln_b,
              o_w_out, o_norm_post):
    for layer in range(DEPTH):
        i = layer // 2
        if layer % 2 == 0:
            x = ssd_fox_layer(x, e_norm_pre[i], e_w_in[i], e_conv_w[i], e_conv_b[i], e_dt_bias[i],
                              e_a_log[i], e_d_skip[i], e_fgate_b[i], e_ssd_norm[i], e_w_out[i],
                              e_norm_post[i])
        else:
            x = conformer_conv_layer(x, o_norm_pre[i], o_w_in[i], o_conv_w[i], o_conv_b[i], o_ln_g[i],
                                     o_ln_b[i], o_w_out[i], o_norm_post[i])
    return x
```

```python
import functools

import jax
import jax.numpy as jnp
from jax import lax
from jax.experimental import pallas as pl
from jax.experimental.pallas import tpu as pltpu

F32 = jnp.float32
BF16 = jnp.bfloat16

LANES = 128
D_MODEL = 1024
SSD_HEADS = 16
SSD_GROUPS = 4
HEADDIM = 64
D_STATE = 128
CHUNK = 128
SSD_CONV = 4
FOX_HEADS = 16
D_CONV = 2 * D_MODEL
CONV_WIDTH = 31
EPS = 1e-6
NEG_BIG = -1e30
VMEM_LIMIT = 56 * 1024 * 1024

CB = D_MODEL // LANES


def _dot(a, b):
    return jnp.dot(a, b, preferred_element_type=F32)


def _dot_nt(a, b):
    return lax.dot_general(a, b, (((1,), (1,)), ((), ())), preferred_element_type=F32)


def _dot_tn(a, b):
    return lax.dot_general(a, b, (((0,), (0,)), ((), ())), preferred_element_type=F32)


def _split(a, n):
    pieces = []
    r = a
    for _ in range(n):
        p = r.astype(BF16)
        pieces.append(p)
        r = r - p.astype(F32)
    return pieces


def _dot_split_lhs(a, m_bf16, n):
    out = None
    for p in _split(a, n):
        t = _dot(p, m_bf16)
        out = t if out is None else out + t
    return out


def _dot_split_rhs(m_bf16, a, n):
    out = None
    for p in _split(a, n):
        t = _dot(m_bf16, p)
        out = t if out is None else out + t
    return out


def _sigmoid(x):
    return 1.0 / (1.0 + jnp.exp(-x))


def _silu(x):
    return x * _sigmoid(x)


def _softplus(x):
    return jnp.maximum(x, 0.0) + jnp.log(1.0 + jnp.exp(-jnp.abs(x)))


def _log_sigmoid(x):
    return jnp.minimum(x, 0.0) - jnp.log(1.0 + jnp.exp(-jnp.abs(x)))


def _cat_blocks(ref, lo, hi, rows=None):
    if rows is None:
        return jnp.concatenate([ref[c] for c in range(lo, hi)], axis=-1)
    return jnp.concatenate([ref[c, rows, :] for c in range(lo, hi)], axis=-1)


def _inproj0_kernel(x_ref, g_ref, w_ref, wvt_ref, wsh_ref, wsl_ref,
                    proj_ref, vt_ref, sm_ref, smt_ref, xn_scr, *, ncb):
    j = pl.program_id(1)

    @pl.when(j == 0)
    def _():
        x = x_ref[...]
        ms = jnp.mean(x * x, axis=-1, keepdims=True)
        xn = x * lax.rsqrt(ms + EPS) * g_ref[...]
        hi = xn.astype(BF16)
        lo = (xn - hi.astype(F32)).astype(BF16)
        xn_scr[...] = hi
        small = _dot(hi, wsh_ref[...]) + _dot(lo, wsh_ref[...]) + _dot(hi, wsl_ref[...])
        sm_ref[...] = small
        smt_ref[...] = small.T
        vt = _dot_nt(wvt_ref[...], hi)
        for c in range(CB):
            vt_ref[c] = vt[c * LANES:(c + 1) * LANES, :].astype(BF16)

    acc = _dot(xn_scr[...], w_ref[...])
    for c in range(ncb):
        proj_ref[c] = acc[:, c * LANES:(c + 1) * LANES].astype(BF16)


def _inproj0(x2, g, w_main, wvt, wsh, wsl, *, tm, tn):
    m = x2.shape[0]
    ncol = w_main.shape[1] // tn
    ncb = tn // LANES
    kern = functools.partial(_inproj0_kernel, ncb=ncb)
    return pl.pallas_call(
        kern,
        grid=(m // tm, ncol),
        in_specs=[
            pl.BlockSpec((tm, D_MODEL), lambda i, j: (i, 0)),
            pl.BlockSpec((1, D_MODEL), lambda i, j: (0, 0)),
            pl.BlockSpec((D_MODEL, tn), lambda i, j: (0, j)),
            pl.BlockSpec((D_MODEL, D_MODEL), lambda i, j: (0, 0)),
            pl.BlockSpec((D_MODEL, LANES), lambda i, j: (0, 0)),
            pl.BlockSpec((D_MODEL, LANES), lambda i, j: (0, 0)),
        ],
        out_specs=[
            pl.BlockSpec((ncb, tm, LANES), lambda i, j: (j, i, 0)),
            pl.BlockSpec((CB, LANES, tm), lambda i, j: (0, 0, i)),
            pl.BlockSpec((tm, LANES), lambda i, j: (i, 0)),
            pl.BlockSpec((LANES, tm), lambda i, j: (0, i)),
        ],
        out_shape=[
            jax.ShapeDtypeStruct((w_main.shape[1] // LANES, m, LANES), BF16),
            jax.ShapeDtypeStruct((CB, LANES, m), BF16),
            jax.ShapeDtypeStruct((m, LANES), F32),
            jax.ShapeDtypeStruct((LANES, m), F32),
        ],
        scratch_shapes=[pltpu.VMEM((tm, D_MODEL), BF16)],
        compiler_params=pltpu.CompilerParams(
            dimension_semantics=("arbitrary", "arbitrary"), vmem_limit_bytes=VMEM_LIMIT),
        name="inproj0",
    )(x2, g, w_main, wvt, wsh, wsl)


def _ssd_kernel(xbc_ref, z_ref, sm_ref, smt_ref, convw_ref, convb_ref, prow_ref, pcol_ref,
                dskip_ref, norm_ref, e64_ref,
                y_ref, c_ref,
                h_scr, ext_scr, xc_scr, ccarry_scr, *, T):
    t = pl.program_id(1)
    nxc = 2 * CB

    @pl.when(t == 0)
    def _():
        h_scr[...] = jnp.zeros_like(h_scr)
        ext_scr[0:8, :] = jnp.zeros((8, nxc * LANES), F32)
        ccarry_scr[...] = jnp.zeros_like(ccarry_scr)

    for c in range(nxc):
        ext_scr[8:8 + T, c * LANES:(c + 1) * LANES] = xbc_ref[c].astype(F32)
    for c in range(nxc):
        cs_ = slice(c * LANES, (c + 1) * LANES)
        acc = convb_ref[:, cs_]
        for k in range(SSD_CONV):
            off = 8 - (SSD_CONV - 1) + k
            acc = acc + ext_scr[off:off + T, cs_] * convw_ref[k:k + 1, cs_]
        xc_scr[:, cs_] = _silu(acc)
    ext_scr[0:8, :] = ext_scr[T:T + 8, :]

    dtb_row = prow_ref[0:1, :]
    a_row = -jnp.exp(prow_ref[1:2, :])
    fb_row = prow_ref[2:3, :]
    dtb_col = pcol_ref[0]
    a_col = -jnp.exp(pcol_ref[1])

    rt = lax.broadcasted_iota(jnp.int32, (T, T), 0)
    ct = lax.broadcasted_iota(jnp.int32, (T, T), 1)
    ltri_t = jnp.where(rt >= ct, 1.0, 0.0).astype(BF16)
    logf = _log_sigmoid(sm_ref[...] + fb_row)
    ctile = _dot_split_rhs(ltri_t, logf, 3) + ccarry_scr[...]
    c_ref[...] = ctile
    ccarry_scr[...] = ctile[T - 1:T, :]

    ri = lax.broadcasted_iota(jnp.int32, (CHUNK, CHUNK), 0)
    ci = lax.broadcasted_iota(jnp.int32, (CHUNK, CHUNK), 1)
    causal = ri >= ci
    ltri = jnp.where(causal, 1.0, 0.0).astype(BF16)
    utri = jnp.where(ri <= ci, 1.0, 0.0).astype(BF16)
    lane = lax.broadcasted_iota(jnp.int32, (CHUNK, LANES), 1)
    first_head = lane < HEADDIM
    e64 = e64_ref[...]

    for ck in range(T // CHUNK):
        r = slice(ck * CHUNK, (ck + 1) * CHUNK)
        dt = _softplus(sm_ref[r, :] + dtb_row)
        cs = _dot_split_rhs(ltri, dt * a_row, 3)
        dtt = _softplus(smt_ref[0:SSD_HEADS, r] + dtb_col)
        cst = _dot_split_lhs(dtt * a_col, utri, 3)
        cs_e = _dot_split_lhs(cs, e64, 3)
        dt_e = _dot_split_lhs(dt, e64, 3)
        ecs_e = jnp.exp(cs_e)
        dte_e = jnp.exp(cs_e[CHUNK - 1:CHUNK, :] - cs_e)

        for g in range(SSD_GROUPS):
            bsl = slice(D_MODEL + g * D_STATE, D_MODEL + (g + 1) * D_STATE)
            csl = slice(D_MODEL + (SSD_GROUPS + g) * D_STATE,
                        D_MODEL + (SSD_GROUPS + g + 1) * D_STATE)
            bg = xc_scr[r, bsl].astype(BF16)
            cg = xc_scr[r, csl].astype(BF16)
            cb_ = _dot_nt(cg, bg)
            ys = []
            for pp in range(2):
                p = 2 * g + pp
                ps = slice(p * LANES, (p + 1) * LANES)
                x = xc_scr[r, ps]
                xd = x * dt_e[:, ps]
                xd_b = xd.astype(BF16)
                yd = []
                for hh in range(2):
                    h = 2 * p + hh
                    colb = jnp.broadcast_to(cs[:, h:h + 1], (CHUNK, CHUNK))
                    rowb = jnp.broadcast_to(cst[h:h + 1, :], (CHUNK, CHUNK))
                    lm = jnp.where(causal, jnp.exp(colb - rowb), 0.0)
                    yd.append(_dot((cb_ * lm).astype(BF16), xd_b))
                ydiag = jnp.where(first_head, yd[0], yd[1])
                hin = h_scr[:, ps]
                yoff = _dot(cg, hin.astype(BF16)) * ecs_e[:, ps]
                snew = _dot_tn(bg, (xd * dte_e[:, ps]).astype(BF16))
                h_scr[:, ps] = hin * ecs_e[CHUNK - 1:CHUNK, ps] + snew
                yp = ydiag + yoff + x * dskip_ref[:, ps]
                yp = yp * _silu(z_ref[p, r, :].astype(F32))
                ys.append(yp)
            ss = (jnp.sum(ys[0] * ys[0], axis=-1, keepdims=True)
                  + jnp.sum(ys[1] * ys[1], axis=-1, keepdims=True))
            scale = lax.rsqrt(ss * (1.0 / (2 * LANES)) + EPS)
            for pp in range(2):
                p = 2 * g + pp
                ps = slice(p * LANES, (p + 1) * LANES)
                y_ref[p, r, :] = (ys[pp] * scale * norm_ref[:, ps]).astype(BF16)


def _ssd(proj, sm, smt, convw, convb, prow, pcol, dskip_e, norm, e64, *, bsz, seq, T):
    m = bsz * seq
    nt = seq // T
    kern = functools.partial(_ssd_kernel, T=T)
    const2 = lambda b, t: (0, 0)
    return pl.pallas_call(
        kern,
        grid=(bsz, nt),
        in_specs=[
            pl.BlockSpec((2 * CB, T, LANES), lambda b, t: (1, b * nt + t, 0)),
            pl.BlockSpec((CB, T, LANES), lambda b, t: (0, b * nt + t, 0)),
            pl.BlockSpec((T, LANES), lambda b, t: (b * nt + t, 0)),
            pl.BlockSpec((LANES, T), lambda b, t: (0, b * nt + t)),
            pl.BlockSpec(convw.shape, const2),
            pl.BlockSpec(convb.shape, const2),
            pl.BlockSpec(prow.shape, const2),
            pl.BlockSpec(pcol.shape, lambda b, t: (0, 0, 0)),
            pl.BlockSpec(dskip_e.shape, const2),
            pl.BlockSpec(norm.shape, const2),
            pl.BlockSpec(e64.shape, const2),
        ],
        out_specs=[
            pl.BlockSpec((CB, T, LANES), lambda b, t: (0, b * nt + t, 0)),
            pl.BlockSpec((T, LANES), lambda b, t: (b * nt + t, 0)),
        ],
        out_shape=[
            jax.ShapeDtypeStruct((CB, m, LANES), BF16),
            jax.ShapeDtypeStruct((m, LANES), F32),
        ],
        scratch_shapes=[
            pltpu.VMEM((D_STATE, D_MODEL), F32),
            pltpu.VMEM((T + 8, 2 * D_MODEL), F32),
            pltpu.VMEM((T, 2 * D_MODEL), F32),
            pltpu.VMEM((1, LANES), F32),
        ],
        compiler_params=pltpu.CompilerParams(
            dimension_semantics=("arbitrary", "arbitrary"), vmem_limit_bytes=VMEM_LIMIT),
        name="ssd",
    )(proj, proj, sm, smt, convw, convb, prow, pcol, dskip_e, norm, e64)


def _fox_kernel(q_ref, k_ref, vt_ref, cq_ref, c_ref, o_ref, m_scr, l_scr, acc_scr, *,
                tq, tk, c_lane0):
    pair = pl.program_id(1)
    i = pl.program_id(2)
    lane_k = lax.broadcasted_iota(jnp.int32, (tk, LANES), 1)
    lane = lax.broadcasted_iota(jnp.int32, (tq, LANES), 1)
    q = q_ref[...] * jnp.asarray(HEADDIM ** -0.5, BF16)
    zero = jnp.zeros_like(q)
    qh = (jnp.where(lane < HEADDIM, q, zero), jnp.where(lane >= HEADDIM, q, zero))

    m_scr[...] = jnp.full_like(m_scr, NEG_BIG)
    l_scr[...] = jnp.zeros_like(l_scr)
    acc_scr[...] = jnp.zeros_like(acc_scr)

    def block(j, masked):
        koff = pl.multiple_of(j * tk, tk)
        k = k_ref[pl.ds(koff, tk), :]
        cblk = c_ref[pl.ds(koff, tk), :]
        if masked:
            kv_pos = koff + lax.broadcasted_iota(jnp.int32, (tk, tq), 0)
            q_pos = i * tq + lax.broadcasted_iota(jnp.int32, (tk, tq), 1)
            valid = kv_pos <= q_pos
        for hh in range(2):
            s = _dot_nt(k, qh[hh])
            sel = lane_k == (c_lane0 + 2 * pair + hh)
            ck_col = jnp.sum(jnp.where(sel, cblk, 0.0), axis=1, keepdims=True)
            s = s - ck_col + cq_ref[hh:hh + 1, :]
            if masked:
                s = jnp.where(valid, s, NEG_BIG)
            m_prev = m_scr[hh]
            m_new = jnp.maximum(m_prev, jnp.max(s, axis=0, keepdims=True))
            alpha = jnp.exp(m_prev - m_new)
            p = jnp.exp(s - m_new)
            l_scr[hh] = alpha * l_scr[hh] + jnp.sum(p, axis=0, keepdims=True)
            m_scr[hh] = m_new
            vt = vt_ref[hh * HEADDIM:(hh + 1) * HEADDIM, pl.ds(koff, tk)]
            acc_scr[hh] = acc_scr[hh] * alpha + _dot(vt, p.astype(BF16))

    nfull = (i * tq) // tk

    def body(j, carry):
        block(j, masked=False)
        return carry

    lax.fori_loop(0, nfull, body, 0)
    for d in range(tq // tk):
        block(nfull + d, masked=True)

    out = jnp.concatenate(
        [acc_scr[hh] * (1.0 / l_scr[hh]) for hh in range(2)], axis=0)
    o_ref[...] = out.T.astype(BF16)


def _fox(proj, vt, c8, c, *, bsz, seq, tq, tk, q_cb0, k_cb0, c_lane0):
    m = bsz * seq
    nq = seq // tq
    npair = FOX_HEADS // 2
    kern = functools.partial(_fox_kernel, tq=tq, tk=tk, c_lane0=c_lane0)
    return pl.pallas_call(
        kern,
        grid=(bsz, npair, nq),
        in_specs=[
            pl.BlockSpec((None, tq, LANES), lambda b, p, i: (q_cb0 + p, b * nq + i, 0)),
            pl.BlockSpec((None, seq, LANES), lambda b, p, i: (k_cb0 + p, b, 0)),
            pl.BlockSpec((None, LANES, seq), lambda b, p, i: (p, 0, b)),
            pl.BlockSpec((None, 8, tq), lambda b, p, i: (p, 0, b * nq + i)),
            pl.BlockSpec((seq, LANES), lambda b, p, i: (b, 0)),
        ],
        out_specs=pl.BlockSpec((None, tq, LANES), lambda b, p, i: (p, b * nq + i, 0)),
        out_shape=jax.ShapeDtypeStruct((npair, m, LANES), BF16),
        scratch_shapes=[
            pltpu.VMEM((2, 1, tq), F32),
            pltpu.VMEM((2, 1, tq), F32),
            pltpu.VMEM((2, HEADDIM, tq), F32),
        ],
        compiler_params=pltpu.CompilerParams(
            dimension_semantics=("arbitrary", "arbitrary", "arbitrary"),
            vmem_limit_bytes=VMEM_LIMIT),
        name="fox",
    )(proj, proj, vt, c8, c)


def _outproj0_kernel(y_ref, o_ref, z_ref, x_ref, w_ref, g_ref, out_ref):
    y = _cat_blocks(y_ref, 0, CB)
    o = _cat_blocks(o_ref, 0, CB).astype(F32)
    z = _cat_blocks(z_ref, 0, CB).astype(F32)
    og = (o * _silu(z)).astype(BF16)
    acc = _dot(y, w_ref[0:D_MODEL, :]) + _dot(og, w_ref[D_MODEL:2 * D_MODEL, :])
    ms = jnp.mean(acc * acc, axis=-1, keepdims=True)
    out_ref[...] = x_ref[...] + acc * lax.rsqrt(ms + EPS) * g_ref[...]


def _outproj0(y, o, proj, x2, w, g, *, tm):
    m = x2.shape[0]
    return pl.pallas_call(
        _outproj0_kernel,
        grid=(m // tm,),
        in_specs=[
            pl.BlockSpec((CB, tm, LANES), lambda i: (0, i, 0)),
            pl.BlockSpec((CB, tm, LANES), lambda i: (0, i, 0)),
            pl.BlockSpec((CB, tm, LANES), lambda i: (1, i, 0)),
            pl.BlockSpec((tm, D_MODEL), lambda i: (i, 0)),
            pl.BlockSpec(w.shape, lambda i: (0, 0)),
            pl.BlockSpec((1, D_MODEL), lambda i: (0, 0)),
        ],
        out_specs=pl.BlockSpec((tm, D_MODEL), lambda i: (i, 0)),
        out_shape=jax.ShapeDtypeStruct((m, D_MODEL), F32),
        compiler_params=pltpu.CompilerParams(
            dimension_semantics=("arbitrary",), vmem_limit_bytes=VMEM_LIMIT),
        name="outproj0",
    )(y, o, proj, x2, w, g)


def _inproj1_kernel(x_ref, g_ref, w_ref, h_ref, z_ref, xn_scr, *, tn, nglu):
    j = pl.program_id(1)
    half = tn // 2

    @pl.when(j == 0)
    def _():
        x = x_ref[...]
        ms = jnp.mean(x * x, axis=-1, keepdims=True)
        xn_scr[...] = (x * lax.rsqrt(ms + EPS) * g_ref[...]).astype(BF16)

    acc = _dot(xn_scr[...], w_ref[...])

    @pl.when(j < nglu)
    def _():
        hv = acc[:, 0:half] * _sigmoid(acc[:, half:tn])
        for c in range(half // LANES):
            h_ref[c] = hv[:, c * LANES:(c + 1) * LANES].astype(BF16)

    @pl.when(j >= nglu)
    def _():
        for c in range(tn // LANES):
            z_ref[c] = acc[:, c * LANES:(c + 1) * LANES].astype(BF16)


def _inproj1(x2, g, w, *, tm, tn):
    m = x2.shape[0]
    ncol = w.shape[1] // tn
    nglu = (2 * D_CONV) // tn
    hcb = tn // 2 // LANES
    zcb = tn // LANES
    kern = functools.partial(_inproj1_kernel, tn=tn, nglu=nglu)
    return pl.pallas_call(
        kern,
        grid=(m // tm, ncol),
        in_specs=[
            pl.BlockSpec((tm, D_MODEL), lambda i, j: (i, 0)),
            pl.BlockSpec((1, D_MODEL), lambda i, j: (0, 0)),
            pl.BlockSpec((D_MODEL, tn), lambda i, j: (0, j)),
        ],
        out_specs=[
            pl.BlockSpec((hcb, tm, LANES), lambda i, j: (jnp.minimum(j, nglu - 1), i, 0)),
            pl.BlockSpec((zcb, tm, LANES), lambda i, j: (jnp.maximum(j - nglu, 0), i, 0)),
        ],
        out_shape=[
            jax.ShapeDtypeStruct((D_CONV // LANES, m, LANES), BF16),
            jax.ShapeDtypeStruct((D_CONV // LANES, m, LANES), BF16),
        ],
        scratch_shapes=[pltpu.VMEM((tm, D_MODEL), BF16)],
        compiler_params=pltpu.CompilerParams(
            dimension_semantics=("arbitrary", "arbitrary"), vmem_limit_bytes=VMEM_LIMIT),
        name="inproj1",
    )(x2, g, w)


def _conv1_kernel(h_ref, z_ref, x_ref, cw_ref, cb_ref, lg_ref, lb_ref, w_ref, g_ref,
                  out_ref, ext_scr, hc_scr, *, T):
    t = pl.program_id(1)
    ncb = D_CONV // LANES
    halo = 32

    @pl.when(t == 0)
    def _():
        ext_scr[0:halo, :] = jnp.zeros((halo, D_CONV), F32)

    for c in range(ncb):
        ext_scr[halo:halo + T, c * LANES:(c + 1) * LANES] = h_ref[c].astype(F32)
    for c in range(ncb):
        cs_ = slice(c * LANES, (c + 1) * LANES)
        acc = cb_ref[:, cs_]
        for k in range(CONV_WIDTH):
            off = halo - (CONV_WIDTH - 1) + k
            acc = acc + ext_scr[off:off + T, cs_] * cw_ref[k:k + 1, cs_]
        hc_scr[:, cs_] = acc
    ext_scr[0:halo, :] = ext_scr[T:T + halo, :]

    hc = hc_scr[...]
    mu = jnp.mean(hc, axis=-1, keepdims=True)
    xc = hc - mu
    var = jnp.mean(xc * xc, axis=-1, keepdims=True)
    hn = xc * lax.rsqrt(var + EPS) * lg_ref[...] + lb_ref[...]
    z = _cat_blocks(z_ref, 0, ncb).astype(F32)
    hg = (_silu(hn) * _silu(z)).astype(BF16)
    acc = _dot(hg, w_ref[...])
    ms = jnp.mean(acc * acc, axis=-1, keepdims=True)
    out_ref[...] = x_ref[...] + acc * lax.rsqrt(ms + EPS) * g_ref[...]


def _conv1(h, z, x2, cw, cb, lg, lb, w, g, *, bsz, seq, T):
    m = bsz * seq
    nt = seq // T
    ncb = D_CONV // LANES
    kern = functools.partial(_conv1_kernel, T=T)
    const2 = lambda b, t: (0, 0)
    return pl.pallas_call(
        kern,
        grid=(bsz, nt),
        in_specs=[
            pl.BlockSpec((ncb, T, LANES), lambda b, t: (0, b * nt + t, 0)),
            pl.BlockSpec((ncb, T, LANES), lambda b, t: (0, b * nt + t, 0)),
            pl.BlockSpec((T, D_MODEL), lambda b, t: (b * nt + t, 0)),
            pl.BlockSpec(cw.shape, const2),
            pl.BlockSpec(cb.shape, const2),
            pl.BlockSpec(lg.shape, const2),
            pl.BlockSpec(lb.shape, const2),
            pl.BlockSpec(w.shape, const2),
            pl.BlockSpec(g.shape, const2),
        ],
        out_specs=pl.BlockSpec((T, D_MODEL), lambda b, t: (b * nt + t, 0)),
        out_shape=jax.ShapeDtypeStruct((m, D_MODEL), F32),
        scratch_shapes=[
            pltpu.VMEM((T + 32, D_CONV), F32),
            pltpu.VMEM((T, D_CONV), F32),
        ],
        compiler_params=pltpu.CompilerParams(
            dimension_semantics=("arbitrary", "arbitrary"), vmem_limit_bytes=VMEM_LIMIT),
        name="conv1",
    )(h, z, x2, cw, cb, lg, lb, w, g)


def _pad_lanes(v, width=LANES):
    return jnp.pad(v, (0, width - v.shape[0]))


def _even_layer(x2, bsz, seq, g_pre, w_in, conv_w, conv_b, dt_bias, a_log, d_skip, fgate_b,
                ssd_norm, w_out, g_post):
    d_xbc = D_MODEL + 2 * SSD_GROUPS * D_STATE
    o_z, o_xbc = 0, 2 * D_MODEL
    o_dt = o_xbc + d_xbc
    o_q = o_dt + SSD_HEADS
    o_k = o_q + D_MODEL
    o_v = o_k + D_MODEL
    o_f = o_v + D_MODEL
    w_main = jnp.concatenate(
        [w_in[:, o_z:o_dt], w_in[:, o_q:o_v]], axis=1).astype(BF16)
    wvt = w_in[:, o_v:o_f].T.astype(BF16)
    w_small = jnp.concatenate(
        [w_in[:, o_dt:o_q], w_in[:, o_f:o_f + FOX_HEADS],
         jnp.zeros((D_MODEL, LANES - SSD_HEADS - FOX_HEADS), F32)], axis=1)
    wsh = w_small.astype(BF16)
    wsl = (w_small - wsh.astype(F32)).astype(BF16)

    proj, vt, sm, smt = _inproj0(x2, g_pre[None, :], w_main, wvt, wsh, wsl, tm=1024, tn=1024)

    zeros16 = jnp.zeros((SSD_HEADS,), F32)
    prow = jnp.stack([
        _pad_lanes(dt_bias),
        _pad_lanes(a_log),
        _pad_lanes(jnp.concatenate([zeros16, fgate_b])),
    ] + [jnp.zeros((LANES,), F32)] * 5)
    pcol = jnp.stack([jnp.broadcast_to(dt_bias[:, None], (SSD_HEADS, LANES)),
                      jnp.broadcast_to(a_log[:, None], (SSD_HEADS, LANES))])
    dskip_e = jnp.repeat(d_skip, HEADDIM)[None, :]
    head_of_lane = jnp.arange(D_MODEL) // HEADDIM
    e64 = (jnp.arange(LANES)[:, None] == head_of_lane[None, :]).astype(BF16)

    y, c = _ssd(proj, sm, smt, conv_w, conv_b[None, :], prow, pcol, dskip_e,
                ssd_norm[None, :], e64, bsz=bsz, seq=seq, T=256)

    ct = c[:, SSD_HEADS:SSD_HEADS + FOX_HEADS].T.reshape(FOX_HEADS // 2, 2, -1)
    c8 = jnp.pad(ct, ((0, 0), (0, 6), (0, 0)))

    o = _fox(proj, vt, c8, c, bsz=bsz, seq=seq, tq=512, tk=512,
             q_cb0=4 * CB, k_cb0=5 * CB, c_lane0=SSD_HEADS)
    return _outproj0(y, o, proj, x2, w_out.astype(BF16), g_post[None, :], tm=512)


def _odd_layer(x2, bsz, seq, g_pre, w_in, conv_w, conv_b, ln_g, ln_b, w_out, g_post):
    tn = 1024
    half = tn // 2
    nglu = (2 * D_CONV) // tn
    cols = []
    for j in range(nglu):
        cols.append(w_in[:, j * half:(j + 1) * half])
        cols.append(w_in[:, D_CONV + j * half:D_CONV + (j + 1) * half])
    cols.append(w_in[:, 2 * D_CONV:])
    w = jnp.concatenate(cols, axis=1).astype(BF16)
    h, z = _inproj1(x2, g_pre[None, :], w, tm=1024, tn=tn)
    return _conv1(h, z, x2, conv_w, conv_b[None, :], ln_g[None, :], ln_b[None, :],
                  w_out.astype(BF16), g_post[None, :], bsz=bsz, seq=seq, T=256)


def kernel(x, e_norm_pre, e_w_in, e_conv_w, e_conv_b, e_dt_bias, e_a_log, e_d_skip, e_fgate_b,
           e_ssd_norm, e_w_out, e_norm_post, o_norm_pre, o_w_in, o_conv_w, o_conv_b, o_ln_g,
           o_ln_b, o_w_out, o_norm_post):
    bsz, seq, d = x.shape
    x2 = x.reshape(bsz * seq, d)
    depth = e_w_in.shape[0] + o_w_in.shape[0]
    for layer in range(depth):
        i = layer // 2
        if layer % 2 == 0:
            x2 = _even_layer(x2, bsz, seq, e_norm_pre[i], e_w_in[i], e_conv_w[i], e_conv_b[i],
                             e_dt_bias[i], e_a_log[i], e_d_skip[i], e_fgate_b[i],
                             e_ssd_norm[i], e_w_out[i], e_norm_post[i])
        else:
            x2 = _odd_layer(x2, bsz, seq, o_norm_pre[i], o_w_in[i], o_conv_w[i], o_conv_b[i],
                            o_ln_g[i], o_ln_b[i], o_w_out[i], o_norm_post[i])
    return x2.reshape(bsz, seq, d)
```

```python
import functools

import jax
import jax.numpy as jnp
from jax import lax
from jax.experimental import pallas as pl
from jax.experimental.pallas import tpu as pltpu

F32 = jnp.float32
BF16 = jnp.bfloat16

LANES = 128
SUBLANES = 8
D_MODEL = 1024
SSD_HEADS = 16
SSD_GROUPS = 4
HEADDIM = 64
D_STATE = 128
CHUNK = 128
SSD_CONV = 4
FOX_HEADS = 16
D_CONV = 2 * D_MODEL
CONV_WIDTH = 31
EPS = 1e-6
NEG_BIG = -1e30
LOG2E = 1.4426950408889634
STRIP = 256
ONES_ROWS = 16
LOOKAHEAD = 3
VMEM_LIMIT = 56 * 1024 * 1024

CB = D_MODEL // LANES


def _dot(a, b):
    return jnp.dot(a, b, preferred_element_type=F32)


def _dot_nt(a, b):
    return lax.dot_general(a, b, (((1,), (1,)), ((), ())), preferred_element_type=F32)


def _dot_tn(a, b):
    return lax.dot_general(a, b, (((0,), (0,)), ((), ())), preferred_element_type=F32)


def _split(a, n):
    pieces = []
    r = a
    for _ in range(n):
        p = r.astype(BF16)
        pieces.append(p)
        r = r - p.astype(F32)
    return pieces


def _dot_split_lhs(a, m_bf16, n):
    out = None
    for p in _split(a, n):
        t = _dot(p, m_bf16)
        out = t if out is None else out + t
    return out


def _dot_split_rhs(m_bf16, a, n):
    out = None
    for p in _split(a, n):
        t = _dot(m_bf16, p)
        out = t if out is None else out + t
    return out


def _sigmoid(x):
    return 1.0 / (1.0 + jnp.exp(-x))


def _silu(x):
    return x * _sigmoid(x)


def _softplus(x):
    return jnp.maximum(x, 0.0) + jnp.log(1.0 + jnp.exp(-jnp.abs(x)))


def _log_sigmoid(x):
    return jnp.minimum(x, 0.0) - jnp.log(1.0 + jnp.exp(-jnp.abs(x)))


def _cat_blocks(ref, lo, hi, rows=None):
    if rows is None:
        return jnp.concatenate([ref[c] for c in range(lo, hi)], axis=-1)
    return jnp.concatenate([ref[c, rows, :] for c in range(lo, hi)], axis=-1)


def _inproj0_kernel(x_ref, g_ref, w_ref, wvt_ref, wsh_ref, wsl_ref,
                    proj_ref, vt_ref, sm_ref, smt_ref, xn_scr, *, ncb):
    j = pl.program_id(1)

    @pl.when(j == 0)
    def _():
        x = x_ref[...]
        ms = jnp.mean(x * x, axis=-1, keepdims=True)
        xn = x * lax.rsqrt(ms + EPS) * g_ref[...]
        hi = xn.astype(BF16)
        lo = (xn - hi.astype(F32)).astype(BF16)
        xn_scr[...] = hi
        small = _dot(hi, wsh_ref[...]) + _dot(lo, wsh_ref[...]) + _dot(hi, wsl_ref[...])
        sm_ref[...] = small
        smt_ref[...] = small.T
        vt = _dot_nt(wvt_ref[...], hi)
        for c in range(CB):
            vt_ref[c] = vt[c * LANES:(c + 1) * LANES, :].astype(BF16)

    acc = _dot(xn_scr[...], w_ref[...])
    for c in range(ncb):
        proj_ref[c] = acc[:, c * LANES:(c + 1) * LANES].astype(BF16)


def _inproj0(x2, g, w_main, wvt, wsh, wsl, *, tm, tn):
    m = x2.shape[0]
    ncol = w_main.shape[1] // tn
    ncb = tn // LANES
    kern = functools.partial(_inproj0_kernel, ncb=ncb)
    return pl.pallas_call(
        kern,
        grid=(m // tm, ncol),
        in_specs=[
            pl.BlockSpec((tm, D_MODEL), lambda i, j: (i, 0)),
            pl.BlockSpec((1, D_MODEL), lambda i, j: (0, 0)),
            pl.BlockSpec((D_MODEL, tn), lambda i, j: (0, j)),
            pl.BlockSpec((D_MODEL, D_MODEL), lambda i, j: (0, 0)),
            pl.BlockSpec((D_MODEL, LANES), lambda i, j: (0, 0)),
            pl.BlockSpec((D_MODEL, LANES), lambda i, j: (0, 0)),
        ],
        out_specs=[
            pl.BlockSpec((ncb, tm, LANES), lambda i, j: (j, i, 0)),
            pl.BlockSpec((CB, LANES, tm), lambda i, j: (0, 0, i)),
            pl.BlockSpec((tm, LANES), lambda i, j: (i, 0)),
            pl.BlockSpec((LANES, tm), lambda i, j: (0, i)),
        ],
        out_shape=[
            jax.ShapeDtypeStruct((w_main.shape[1] // LANES, m, LANES), BF16),
            jax.ShapeDtypeStruct((CB, LANES, m), BF16),
            jax.ShapeDtypeStruct((m, LANES), F32),
            jax.ShapeDtypeStruct((LANES, m), F32),
        ],
        scratch_shapes=[pltpu.VMEM((tm, D_MODEL), BF16)],
        compiler_params=pltpu.CompilerParams(
            dimension_semantics=("arbitrary", "arbitrary"), vmem_limit_bytes=VMEM_LIMIT),
        name="inproj0",
    )(x2, g, w_main, wvt, wsh, wsl)


def _ssd_kernel(xbc_ref, z_ref, sm_ref, smt_ref, convw_ref, convb_ref, prow_ref, pcol_ref,
                dskip_ref, norm_ref, e64_ref,
                y_ref, c_ref,
                h_scr, ext_scr, xc_scr, ccarry_scr, *, T):
    t = pl.program_id(1)
    nxc = 2 * CB

    @pl.when(t == 0)
    def _():
        h_scr[...] = jnp.zeros_like(h_scr)
        ext_scr[0:8, :] = jnp.zeros((8, nxc * LANES), F32)
        ccarry_scr[...] = jnp.zeros_like(ccarry_scr)

    for c in range(nxc):
        ext_scr[8:8 + T, c * LANES:(c + 1) * LANES] = xbc_ref[c].astype(F32)
    for c in range(nxc):
        cs_ = slice(c * LANES, (c + 1) * LANES)
        acc = convb_ref[:, cs_]
        for k in range(SSD_CONV):
            off = 8 - (SSD_CONV - 1) + k
            acc = acc + ext_scr[off:off + T, cs_] * convw_ref[k:k + 1, cs_]
        xc_scr[:, cs_] = _silu(acc)
    ext_scr[0:8, :] = ext_scr[T:T + 8, :]

    dtb_row = prow_ref[0:1, :]
    a_row = -jnp.exp(prow_ref[1:2, :])
    fb_row = prow_ref[2:3, :]
    dtb_col = pcol_ref[0]
    a_col = -jnp.exp(pcol_ref[1])

    rt = lax.broadcasted_iota(jnp.int32, (T, T), 0)
    ct = lax.broadcasted_iota(jnp.int32, (T, T), 1)
    ltri_t = jnp.where(rt >= ct, 1.0, 0.0).astype(BF16)
    logf = _log_sigmoid(sm_ref[...] + fb_row)
    ctile = _dot_split_rhs(ltri_t, logf, 3) + ccarry_scr[...]
    c_ref[...] = ctile
    ccarry_scr[...] = ctile[T - 1:T, :]

    ri = lax.broadcasted_iota(jnp.int32, (CHUNK, CHUNK), 0)
    ci = lax.broadcasted_iota(jnp.int32, (CHUNK, CHUNK), 1)
    causal = ri >= ci
    ltri = jnp.where(causal, 1.0, 0.0).astype(BF16)
    utri = jnp.where(ri <= ci, 1.0, 0.0).astype(BF16)
    lane = lax.broadcasted_iota(jnp.int32, (CHUNK, LANES), 1)
    first_head = lane < HEADDIM
    e64 = e64_ref[...]

    for ck in range(T // CHUNK):
        r = slice(ck * CHUNK, (ck + 1) * CHUNK)
        dt = _softplus(sm_ref[r, :] + dtb_row)
        cs = _dot_split_rhs(ltri, dt * a_row, 3)
        dtt = _softplus(smt_ref[0:SSD_HEADS, r] + dtb_col)
        cst = _dot_split_lhs(dtt * a_col, utri, 3)
        cs_e = _dot_split_lhs(cs, e64, 3)
        dt_e = _dot_split_lhs(dt, e64, 3)
        ecs_e = jnp.exp(cs_e)
        dte_e = jnp.exp(cs_e[CHUNK - 1:CHUNK, :] - cs_e)

        for g in range(SSD_GROUPS):
            bsl = slice(D_MODEL + g * D_STATE, D_MODEL + (g + 1) * D_STATE)
            csl = slice(D_MODEL + (SSD_GROUPS + g) * D_STATE,
                        D_MODEL + (SSD_GROUPS + g + 1) * D_STATE)
            bg = xc_scr[r, bsl].astype(BF16)
            cg = xc_scr[r, csl].astype(BF16)
            cb_ = _dot_nt(cg, bg)
            ys = []
            for pp in range(2):
                p = 2 * g + pp
                ps = slice(p * LANES, (p + 1) * LANES)
                x = xc_scr[r, ps]
                xd = x * dt_e[:, ps]
                xd_b = xd.astype(BF16)
                yd = []
                for hh in range(2):
                    h = 2 * p + hh
                    colb = jnp.broadcast_to(cs[:, h:h + 1], (CHUNK, CHUNK))
                    rowb = jnp.broadcast_to(cst[h:h + 1, :], (CHUNK, CHUNK))
                    lm = jnp.where(causal, jnp.exp(colb - rowb), 0.0)
                    yd.append(_dot((cb_ * lm).astype(BF16), xd_b))
                ydiag = jnp.where(first_head, yd[0], yd[1])
                hin = h_scr[:, ps]
                yoff = _dot(cg, hin.astype(BF16)) * ecs_e[:, ps]
                snew = _dot_tn(bg, (xd * dte_e[:, ps]).astype(BF16))
                h_scr[:, ps] = hin * ecs_e[CHUNK - 1:CHUNK, ps] + snew
                yp = ydiag + yoff + x * dskip_ref[:, ps]
                yp = yp * _silu(z_ref[p, r, :].astype(F32))
                ys.append(yp)
            ss = (jnp.sum(ys[0] * ys[0], axis=-1, keepdims=True)
                  + jnp.sum(ys[1] * ys[1], axis=-1, keepdims=True))
            scale = lax.rsqrt(ss * (1.0 / (2 * LANES)) + EPS)
            for pp in range(2):
                p = 2 * g + pp
                ps = slice(p * LANES, (p + 1) * LANES)
                y_ref[p, r, :] = (ys[pp] * scale * norm_ref[:, ps]).astype(BF16)


def _ssd(proj, sm, smt, convw, convb, prow, pcol, dskip_e, norm, e64, *, bsz, seq, T):
    m = bsz * seq
    nt = seq // T
    kern = functools.partial(_ssd_kernel, T=T)
    const2 = lambda b, t: (0, 0)
    return pl.pallas_call(
        kern,
        grid=(bsz, nt),
        in_specs=[
            pl.BlockSpec((2 * CB, T, LANES), lambda b, t: (1, b * nt + t, 0)),
            pl.BlockSpec((CB, T, LANES), lambda b, t: (0, b * nt + t, 0)),
            pl.BlockSpec((T, LANES), lambda b, t: (b * nt + t, 0)),
            pl.BlockSpec((LANES, T), lambda b, t: (0, b * nt + t)),
            pl.BlockSpec(convw.shape, const2),
            pl.BlockSpec(convb.shape, const2),
            pl.BlockSpec(prow.shape, const2),
            pl.BlockSpec(pcol.shape, lambda b, t: (0, 0, 0)),
            pl.BlockSpec(dskip_e.shape, const2),
            pl.BlockSpec(norm.shape, const2),
            pl.BlockSpec(e64.shape, const2),
        ],
        out_specs=[
            pl.BlockSpec((CB, T, LANES), lambda b, t: (0, b * nt + t, 0)),
            pl.BlockSpec((T, LANES), lambda b, t: (b * nt + t, 0)),
        ],
        out_shape=[
            jax.ShapeDtypeStruct((CB, m, LANES), BF16),
            jax.ShapeDtypeStruct((m, LANES), F32),
        ],
        scratch_shapes=[
            pltpu.VMEM((D_STATE, D_MODEL), F32),
            pltpu.VMEM((T + 8, 2 * D_MODEL), F32),
            pltpu.VMEM((T, 2 * D_MODEL), F32),
            pltpu.VMEM((1, LANES), F32),
        ],
        compiler_params=pltpu.CompilerParams(
            dimension_semantics=("arbitrary", "arbitrary"), vmem_limit_bytes=VMEM_LIMIT),
        name="ssd",
    )(proj, proj, sm, smt, convw, convb, prow, pcol, dskip_e, norm, e64)


def _fox_kernel(q_ref, k_ref, vt_ref, cq_ref, c_ref, o_ref, qh_scr, m_scr, acc_scr, *,
                tq, tk, c_lane0):
    pair = pl.program_id(1)
    i = pl.program_id(2)
    nstrip = tq // STRIP
    lane_k = lax.broadcasted_iota(jnp.int32, (tk, LANES), 1)
    lane = lax.broadcasted_iota(jnp.int32, (tq, LANES), 1)
    q = q_ref[...]
    zero = jnp.zeros_like(q)
    qh_scr[0] = jnp.where(lane < HEADDIM, q, zero)
    qh_scr[1] = jnp.where(lane >= HEADDIM, q, zero)
    m_scr[...] = jnp.full_like(m_scr, NEG_BIG)
    acc_scr[...] = jnp.zeros_like(acc_scr)
    ones = jnp.ones((ONES_ROWS, tk), BF16)

    def scores(hh, st, k):
        return _dot_nt(k, qh_scr[hh, st * STRIP:(st + 1) * STRIP, :])

    def softmax_pv(hh, st, s, ck2, vta, koff, partial):
        cols = slice(st * STRIP, (st + 1) * STRIP)
        t = s - ck2
        if partial:
            kv_pos = koff + lax.broadcasted_iota(jnp.int32, (tk, STRIP), 0)
            q_pos = i * tq + st * STRIP + lax.broadcasted_iota(jnp.int32, (tk, STRIP), 1)
            t = jnp.where(kv_pos <= q_pos, t, NEG_BIG)
        cq2 = cq_ref[hh:hh + 1, cols] * LOG2E
        m_prev = m_scr[hh, :, cols]
        m_new = jnp.maximum(m_prev, jnp.max(t, axis=0, keepdims=True) + cq2)
        alpha = jnp.exp2(m_prev - m_new)
        p = jnp.exp2(t - (m_new - cq2)).astype(BF16)
        acc_scr[hh, :, cols] = acc_scr[hh, :, cols] * alpha + _dot(vta, p)
        m_scr[hh, :, cols] = m_new

    def strips_of(diag):
        items = []
        for hh in range(2):
            for st in range(nstrip):
                partial = False
                if diag is not None:
                    kv_lo, q_lo = diag * tk, st * STRIP
                    if kv_lo > q_lo + STRIP - 1:
                        continue
                    partial = kv_lo + tk - 1 > q_lo
                items.append((hh, st, partial))
        return items

    def block(j, diag):
        koff = pl.multiple_of(j * tk, tk)
        k = k_ref[pl.ds(koff, tk), :]
        cblk = c_ref[pl.ds(koff, tk), :]
        ck2, vta = [], []
        for hh in range(2):
            sel = lane_k == (c_lane0 + 2 * pair + hh)
            ck2.append(jnp.sum(jnp.where(sel, cblk, 0.0), axis=1, keepdims=True) * LOG2E)
            vta.append(jnp.concatenate(
                [vt_ref[hh * HEADDIM:(hh + 1) * HEADDIM, pl.ds(koff, tk)], ones], axis=0))
        items = strips_of(diag)
        pending = [scores(it[0], it[1], k) for it in items[:LOOKAHEAD]]
        for n, (hh, st, partial) in enumerate(items):
            if n + LOOKAHEAD < len(items):
                nxt = items[n + LOOKAHEAD]
                pending.append(scores(nxt[0], nxt[1], k))
            softmax_pv(hh, st, pending.pop(0), ck2[hh], vta[hh], koff, partial)

    nfull = (i * tq) // tk

    def body(j, carry):
        block(j, None)
        return carry

    lax.fori_loop(0, nfull, body, 0)
    for d in range(tq // tk):
        block(nfull + d, d)

    outs = []
    for hh in range(2):
        acc = acc_scr[hh]
        outs.append(acc[0:HEADDIM, :] * (1.0 / acc[HEADDIM:HEADDIM + 1, :]))
    o_ref[...] = jnp.concatenate(outs, axis=0).T.astype(BF16)


def _fox(proj, vt, c8, c, *, bsz, seq, tq, tk, q_cb0, k_cb0, c_lane0):
    m = bsz * seq
    nq = seq // tq
    npair = FOX_HEADS // 2
    kern = functools.partial(_fox_kernel, tq=tq, tk=tk, c_lane0=c_lane0)
    return pl.pallas_call(
        kern,
        grid=(bsz, npair, nq),
        in_specs=[
            pl.BlockSpec((None, tq, LANES), lambda b, p, i: (q_cb0 + p, b * nq + i, 0)),
            pl.BlockSpec((None, seq, LANES), lambda b, p, i: (k_cb0 + p, b, 0)),
            pl.BlockSpec((None, LANES, seq), lambda b, p, i: (p, 0, b)),
            pl.BlockSpec((None, 8, tq), lambda b, p, i: (p, 0, b * nq + i)),
            pl.BlockSpec((seq, LANES), lambda b, p, i: (b, 0)),
        ],
        out_specs=pl.BlockSpec((None, tq, LANES), lambda b, p, i: (p, b * nq + i, 0)),
        out_shape=jax.ShapeDtypeStruct((npair, m, LANES), BF16),
        scratch_shapes=[
            pltpu.VMEM((2, tq, LANES), BF16),
            pltpu.VMEM((2, 1, tq), F32),
            pltpu.VMEM((2, HEADDIM + ONES_ROWS, tq), F32),
        ],
        compiler_params=pltpu.CompilerParams(
            dimension_semantics=("arbitrary", "arbitrary", "arbitrary"),
            vmem_limit_bytes=VMEM_LIMIT),
        name="fox",
    )(proj, proj, vt, c8, c)


def _outproj0_kernel(y_ref, o_ref, z_ref, x_ref, w_ref, g_ref, out_ref):
    y = _cat_blocks(y_ref, 0, CB)
    o = _cat_blocks(o_ref, 0, CB).astype(F32)
    z = _cat_blocks(z_ref, 0, CB).astype(F32)
    og = (o * _silu(z)).astype(BF16)
    acc = _dot(y, w_ref[0:D_MODEL, :]) + _dot(og, w_ref[D_MODEL:2 * D_MODEL, :])
    ms = jnp.mean(acc * acc, axis=-1, keepdims=True)
    out_ref[...] = x_ref[...] + acc * lax.rsqrt(ms + EPS) * g_ref[...]


def _outproj0(y, o, proj, x2, w, g, *, tm):
    m = x2.shape[0]
    return pl.pallas_call(
        _outproj0_kernel,
        grid=(m // tm,),
        in_specs=[
            pl.BlockSpec((CB, tm, LANES), lambda i: (0, i, 0)),
            pl.BlockSpec((CB, tm, LANES), lambda i: (0, i, 0)),
            pl.BlockSpec((CB, tm, LANES), lambda i: (1, i, 0)),
            pl.BlockSpec((tm, D_MODEL), lambda i: (i, 0)),
            pl.BlockSpec(w.shape, lambda i: (0, 0)),
            pl.BlockSpec((1, D_MODEL), lambda i: (0, 0)),
        ],
        out_specs=pl.BlockSpec((tm, D_MODEL), lambda i: (i, 0)),
        out_shape=jax.ShapeDtypeStruct((m, D_MODEL), F32),
        compiler_params=pltpu.CompilerParams(
            dimension_semantics=("arbitrary",), vmem_limit_bytes=VMEM_LIMIT),
        name="outproj0",
    )(y, o, proj, x2, w, g)


def _inproj1_kernel(x_ref, g_ref, w_ref, h_ref, z_ref, xn_scr, *, tn, nglu):
    j = pl.program_id(1)
    half = tn // 2

    @pl.when(j == 0)
    def _():
        x = x_ref[...]
        ms = jnp.mean(x * x, axis=-1, keepdims=True)
        xn_scr[...] = (x * lax.rsqrt(ms + EPS) * g_ref[...]).astype(BF16)

    acc = _dot(xn_scr[...], w_ref[...])

    @pl.when(j < nglu)
    def _():
        hv = acc[:, 0:half] * _sigmoid(acc[:, half:tn])
        for c in range(half // LANES):
            h_ref[c] = hv[:, c * LANES:(c + 1) * LANES].astype(BF16)

    @pl.when(j >= nglu)
    def _():
        for c in range(tn // LANES):
            z_ref[c] = acc[:, c * LANES:(c + 1) * LANES].astype(BF16)


def _inproj1(x2, g, w, *, tm, tn):
    m = x2.shape[0]
    ncol = w.shape[1] // tn
    nglu = (2 * D_CONV) // tn
    hcb = tn // 2 // LANES
    zcb = tn // LANES
    kern = functools.partial(_inproj1_kernel, tn=tn, nglu=nglu)
    return pl.pallas_call(
        kern,
        grid=(m // tm, ncol),
        in_specs=[
            pl.BlockSpec((tm, D_MODEL), lambda i, j: (i, 0)),
            pl.BlockSpec((1, D_MODEL), lambda i, j: (0, 0)),
            pl.BlockSpec((D_MODEL, tn), lambda i, j: (0, j)),
        ],
        out_specs=[
            pl.BlockSpec((hcb, tm, LANES), lambda i, j: (jnp.minimum(j, nglu - 1), i, 0)),
            pl.BlockSpec((zcb, tm, LANES), lambda i, j: (jnp.maximum(j - nglu, 0), i, 0)),
        ],
        out_shape=[
            jax.ShapeDtypeStruct((D_CONV // LANES, m, LANES), BF16),
            jax.ShapeDtypeStruct((D_CONV // LANES, m, LANES), BF16),
        ],
        scratch_shapes=[pltpu.VMEM((tm, D_MODEL), BF16)],
        compiler_params=pltpu.CompilerParams(
            dimension_semantics=("arbitrary", "arbitrary"), vmem_limit_bytes=VMEM_LIMIT),
        name="inproj1",
    )(x2, g, w)


def _conv1_kernel(h_ref, z_ref, x_ref, cw_ref, cb_ref, lg_ref, lb_ref, w_ref, g_ref,
                  out_ref, ext_scr, hc_scr, sh_scr, *, T):
    t = pl.program_id(1)
    ncb = D_CONV // LANES
    halo = 32

    @pl.when(t == 0)
    def _():
        ext_scr[0:halo, :] = jnp.zeros((halo, D_CONV), F32)

    for c in range(ncb):
        ext_scr[halo:halo + T, c * LANES:(c + 1) * LANES] = h_ref[c].astype(F32)
    span = T + halo - SUBLANES
    for c in range(ncb):
        cs_ = slice(c * LANES, (c + 1) * LANES)
        for r in range(1, SUBLANES):
            sh_scr[r - 1] = ext_scr[r:r + span, cs_]
        acc = cb_ref[:, cs_]
        for k in range(CONV_WIDTH):
            base, r = divmod(halo - (CONV_WIDTH - 1) + k, SUBLANES)
            base *= SUBLANES
            if r == 0:
                src = ext_scr[base:base + T, cs_]
            else:
                src = sh_scr[r - 1, base:base + T, :]
            acc = acc + src * cw_ref[k:k + 1, cs_]
        hc_scr[:, cs_] = acc
    ext_scr[0:halo, :] = ext_scr[T:T + halo, :]

    hc = hc_scr[...]
    mu = jnp.mean(hc, axis=-1, keepdims=True)
    xc = hc - mu
    var = jnp.mean(xc * xc, axis=-1, keepdims=True)
    hn = xc * lax.rsqrt(var + EPS) * lg_ref[...] + lb_ref[...]
    z = _cat_blocks(z_ref, 0, ncb).astype(F32)
    hg = (_silu(hn) * _silu(z)).astype(BF16)
    acc = _dot(hg, w_ref[...])
    ms = jnp.mean(acc * acc, axis=-1, keepdims=True)
    out_ref[...] = x_ref[...] + acc * lax.rsqrt(ms + EPS) * g_ref[...]


def _conv1(h, z, x2, cw, cb, lg, lb, w, g, *, bsz, seq, T):
    m = bsz * seq
    nt = seq // T
    ncb = D_CONV // LANES
    kern = functools.partial(_conv1_kernel, T=T)
    const2 = lambda b, t: (0, 0)
    return pl.pallas_call(
        kern,
        grid=(bsz, nt),
        in_specs=[
            pl.BlockSpec((ncb, T, LANES), lambda b, t: (0, b * nt + t, 0)),
            pl.BlockSpec((ncb, T, LANES), lambda b, t: (0, b * nt + t, 0)),
            pl.BlockSpec((T, D_MODEL), lambda b, t: (b * nt + t, 0)),
            pl.BlockSpec(cw.shape, const2),
            pl.BlockSpec(cb.shape, const2),
            pl.BlockSpec(lg.shape, const2),
            pl.BlockSpec(lb.shape, const2),
            pl.BlockSpec(w.shape, const2),
            pl.BlockSpec(g.shape, const2),
        ],
        out_specs=pl.BlockSpec((T, D_MODEL), lambda b, t: (b * nt + t, 0)),
        out_shape=jax.ShapeDtypeStruct((m, D_MODEL), F32),
        scratch_shapes=[
            pltpu.VMEM((T + 32, D_CONV), F32),
            pltpu.VMEM((T, D_CONV), F32),
            pltpu.VMEM((SUBLANES - 1, T + 32 - SUBLANES, LANES), F32),
        ],
        compiler_params=pltpu.CompilerParams(
            dimension_semantics=("arbitrary", "arbitrary"), vmem_limit_bytes=VMEM_LIMIT),
        name="conv1",
    )(h, z, x2, cw, cb, lg, lb, w, g)


def _pad_lanes(v, width=LANES):
    return jnp.pad(v, (0, width - v.shape[0]))


def _even_layer(x2, bsz, seq, g_pre, w_in, conv_w, conv_b, dt_bias, a_log, d_skip, fgate_b,
                ssd_norm, w_out, g_post):
    d_xbc = D_MODEL + 2 * SSD_GROUPS * D_STATE
    o_z, o_xbc = 0, 2 * D_MODEL
    o_dt = o_xbc + d_xbc
    o_q = o_dt + SSD_HEADS
    o_k = o_q + D_MODEL
    o_v = o_k + D_MODEL
    o_f = o_v + D_MODEL
    q_scale = HEADDIM ** -0.5 * LOG2E
    w_main = jnp.concatenate(
        [w_in[:, o_z:o_dt], w_in[:, o_q:o_k] * q_scale, w_in[:, o_k:o_v]],
        axis=1).astype(BF16)
    wvt = w_in[:, o_v:o_f].T.astype(BF16)
    w_small = jnp.concatenate(
        [w_in[:, o_dt:o_q], w_in[:, o_f:o_f + FOX_HEADS],
         jnp.zeros((D_MODEL, LANES - SSD_HEADS - FOX_HEADS), F32)], axis=1)
    wsh = w_small.astype(BF16)
    wsl = (w_small - wsh.astype(F32)).astype(BF16)

    proj, vt, sm, smt = _inproj0(x2, g_pre[None, :], w_main, wvt, wsh, wsl, tm=1024, tn=1024)

    zeros16 = jnp.zeros((SSD_HEADS,), F32)
    prow = jnp.stack([
        _pad_lanes(dt_bias),
        _pad_lanes(a_log),
        _pad_lanes(jnp.concatenate([zeros16, fgate_b])),
    ] + [jnp.zeros((LANES,), F32)] * 5)
    pcol = jnp.stack([jnp.broadcast_to(dt_bias[:, None], (SSD_HEADS, LANES)),
                      jnp.broadcast_to(a_log[:, None], (SSD_HEADS, LANES))])
    dskip_e = jnp.repeat(d_skip, HEADDIM)[None, :]
    head_of_lane = jnp.arange(D_MODEL) // HEADDIM
    e64 = (jnp.arange(LANES)[:, None] == head_of_lane[None, :]).astype(BF16)

    y, c = _ssd(proj, sm, smt, conv_w, conv_b[None, :], prow, pcol, dskip_e,
                ssd_norm[None, :], e64, bsz=bsz, seq=seq, T=256)

    ct = c[:, SSD_HEADS:SSD_HEADS + FOX_HEADS].T.reshape(FOX_HEADS // 2, 2, -1)
    c8 = jnp.pad(ct, ((0, 0), (0, 6), (0, 0)))

    o = _fox(proj, vt, c8, c, bsz=bsz, seq=seq, tq=1024, tk=512,
             q_cb0=4 * CB, k_cb0=5 * CB, c_lane0=SSD_HEADS)
    return _outproj0(y, o, proj, x2, w_out.astype(BF16), g_post[None, :], tm=512)


def _odd_layer(x2, bsz, seq, g_pre, w_in, conv_w, conv_b, ln_g, ln_b, w_out, g_post):
    tn = 1024
    half = tn // 2
    nglu = (2 * D_CONV) // tn
    cols = []
    for j in range(nglu):
        cols.append(w_in[:, j * half:(j + 1) * half])
        cols.append(w_in[:, D_CONV + j * half:D_CONV + (j + 1) * half])
    cols.append(w_in[:, 2 * D_CONV:])
    w = jnp.concatenate(cols, axis=1).astype(BF16)
    h, z = _inproj1(x2, g_pre[None, :], w, tm=1024, tn=tn)
    return _conv1(h, z, x2, conv_w, conv_b[None, :], ln_g[None, :], ln_b[None, :],
                  w_out.astype(BF16), g_post[None, :], bsz=bsz, seq=seq, T=256)


def kernel(x, e_norm_pre, e_w_in, e_conv_w, e_conv_b, e_dt_bias, e_a_log, e_d_skip, e_fgate_b,
           e_ssd_norm, e_w_out, e_norm_post, o_norm_pre, o_w_in, o_conv_w, o_conv_b, o_ln_g,
           o_ln_b, o_w_out, o_norm_post):
    bsz, seq, d = x.shape
    x2 = x.reshape(bsz * seq, d)
    depth = e_w_in.shape[0] + o_w_in.shape[0]
    for layer in range(depth):
        i = layer // 2
        if layer % 2 == 0:
            x2 = _even_layer(x2, bsz, seq, e_norm_pre[i], e_w_in[i], e_conv_w[i], e_conv_b[i],
                             e_dt_bias[i], e_a_log[i], e_d_skip[i], e_fgate_b[i],
                             e_ssd_norm[i], e_w_out[i], e_norm_post[i])
        else:
            x2 = _odd_layer(x2, bsz, seq, o_norm_pre[i], o_w_in[i], o_conv_w[i], o_conv_b[i],
                            o_ln_g[i], o_ln_b[i], o_w_out[i], o_norm_post[i])
    return x2.reshape(bsz, seq, d)
```

```python
import functools

import jax
import jax.numpy as jnp
from jax import lax
from jax.experimental import pallas as pl
from jax.experimental.pallas import tpu as pltpu

F32 = jnp.float32
BF16 = jnp.bfloat16

LANES = 128
SUBLANES = 8
D_MODEL = 1024
SSD_HEADS = 16
SSD_GROUPS = 4
HEADDIM = 64
D_STATE = 128
CHUNK = 128
SSD_CONV = 4
FOX_HEADS = 16
D_CONV = 2 * D_MODEL
CONV_WIDTH = 31
EPS = 1e-6
NEG_BIG = -1e30
LOG2E = 1.4426950408889634
STRIP = 256
ONES_ROWS = 16
CONV_HALO = 32
LOOKAHEAD = 4
VMEM_LIMIT = 56 * 1024 * 1024

CB = D_MODEL // LANES


def _dot(a, b):
    return jnp.dot(a, b, preferred_element_type=F32)


def _dot_nt(a, b):
    return lax.dot_general(a, b, (((1,), (1,)), ((), ())), preferred_element_type=F32)


def _dot_tn(a, b):
    return lax.dot_general(a, b, (((0,), (0,)), ((), ())), preferred_element_type=F32)


def _split(a, n):
    pieces = []
    r = a
    for _ in range(n):
        p = r.astype(BF16)
        pieces.append(p)
        r = r - p.astype(F32)
    return pieces


def _dot_split_lhs(a, m_bf16, n):
    out = None
    for p in _split(a, n):
        t = _dot(p, m_bf16)
        out = t if out is None else out + t
    return out


def _dot_split_rhs(m_bf16, a, n):
    out = None
    for p in _split(a, n):
        t = _dot(m_bf16, p)
        out = t if out is None else out + t
    return out


def _sigmoid(x):
    return 1.0 / (1.0 + jnp.exp(-x))


def _silu(x):
    return x * _sigmoid(x)


def _softplus(x):
    return jnp.maximum(x, 0.0) + jnp.log(1.0 + jnp.exp(-jnp.abs(x)))


def _log_sigmoid(x):
    return jnp.minimum(x, 0.0) - jnp.log(1.0 + jnp.exp(-jnp.abs(x)))


def _cat_blocks(ref, lo, hi, rows=None):
    if rows is None:
        return jnp.concatenate([ref[c] for c in range(lo, hi)], axis=-1)
    return jnp.concatenate([ref[c, rows, :] for c in range(lo, hi)], axis=-1)


def _inproj0_kernel(x_ref, g_ref, w_ref, wvt_ref, wsh_ref, wsl_ref,
                    proj_ref, vt_ref, sm_ref, smt_ref, xn_scr, *, ncb):
    j = pl.program_id(1)

    @pl.when(j == 0)
    def _():
        x = x_ref[...]
        ms = jnp.mean(x * x, axis=-1, keepdims=True)
        xn = x * lax.rsqrt(ms + EPS) * g_ref[...]
        hi = xn.astype(BF16)
        lo = (xn - hi.astype(F32)).astype(BF16)
        xn_scr[...] = hi
        small = _dot(hi, wsh_ref[...]) + _dot(lo, wsh_ref[...]) + _dot(hi, wsl_ref[...])
        sm_ref[...] = small
        smt_ref[...] = small.T
        vt = _dot_nt(wvt_ref[...], hi)
        for c in range(CB):
            vt_ref[c] = vt[c * LANES:(c + 1) * LANES, :].astype(BF16)

    acc = _dot(xn_scr[...], w_ref[...])
    for c in range(ncb):
        proj_ref[c] = acc[:, c * LANES:(c + 1) * LANES].astype(BF16)


def _inproj0(x2, g, w_main, wvt, wsh, wsl, *, tm, tn):
    m = x2.shape[0]
    ncol = w_main.shape[1] // tn
    ncb = tn // LANES
    kern = functools.partial(_inproj0_kernel, ncb=ncb)
    return pl.pallas_call(
        kern,
        grid=(m // tm, ncol),
        in_specs=[
            pl.BlockSpec((tm, D_MODEL), lambda i, j: (i, 0)),
            pl.BlockSpec((1, D_MODEL), lambda i, j: (0, 0)),
            pl.BlockSpec((D_MODEL, tn), lambda i, j: (0, j)),
            pl.BlockSpec((D_MODEL, D_MODEL), lambda i, j: (0, 0)),
            pl.BlockSpec((D_MODEL, LANES), lambda i, j: (0, 0)),
            pl.BlockSpec((D_MODEL, LANES), lambda i, j: (0, 0)),
        ],
        out_specs=[
            pl.BlockSpec((ncb, tm, LANES), lambda i, j: (j, i, 0)),
            pl.BlockSpec((CB, LANES, tm), lambda i, j: (0, 0, i)),
            pl.BlockSpec((tm, LANES), lambda i, j: (i, 0)),
            pl.BlockSpec((LANES, tm), lambda i, j: (0, i)),
        ],
        out_shape=[
            jax.ShapeDtypeStruct((w_main.shape[1] // LANES, m, LANES), BF16),
            jax.ShapeDtypeStruct((CB, LANES, m), BF16),
            jax.ShapeDtypeStruct((m, LANES), F32),
            jax.ShapeDtypeStruct((LANES, m), F32),
        ],
        scratch_shapes=[pltpu.VMEM((tm, D_MODEL), BF16)],
        compiler_params=pltpu.CompilerParams(
            dimension_semantics=("arbitrary", "arbitrary"), vmem_limit_bytes=VMEM_LIMIT),
        name="inproj0",
    )(x2, g, w_main, wvt, wsh, wsl)


def _ssd_kernel(xbc_ref, z_ref, sm_ref, smt_ref, convw_ref, convb_ref, prow_ref, pcol_ref,
                dskip_ref, norm_ref, e64_ref,
                y_ref, c_ref,
                h_scr, ext_scr, xc_scr, ccarry_scr, sh_scr, *, T):
    t = pl.program_id(1)
    nxc = 2 * CB

    @pl.when(t == 0)
    def _():
        h_scr[...] = jnp.zeros_like(h_scr)
        ext_scr[0:8, :] = jnp.zeros((8, nxc * LANES), F32)
        ccarry_scr[...] = jnp.zeros_like(ccarry_scr)

    for c in range(nxc):
        ext_scr[8:8 + T, c * LANES:(c + 1) * LANES] = xbc_ref[c].astype(F32)
    for c in range(nxc):
        cs_ = slice(c * LANES, (c + 1) * LANES)
        for k in range(SSD_CONV - 1):
            off = SUBLANES - (SSD_CONV - 1) + k
            sh_scr[k] = ext_scr[off:off + T, cs_]
        acc = convb_ref[:, cs_] + (ext_scr[SUBLANES:SUBLANES + T, cs_]
                                   * convw_ref[SSD_CONV - 1:SSD_CONV, cs_])
        for k in range(SSD_CONV - 1):
            acc = acc + sh_scr[k] * convw_ref[k:k + 1, cs_]
        xc_scr[:, cs_] = _silu(acc)
    ext_scr[0:8, :] = ext_scr[T:T + 8, :]

    dtb_row = prow_ref[0:1, :]
    a_row = -jnp.exp(prow_ref[1:2, :])
    fb_row = prow_ref[2:3, :]
    dtb_col = pcol_ref[0]
    a_col = -jnp.exp(pcol_ref[1])

    rt = lax.broadcasted_iota(jnp.int32, (T, T), 0)
    ct = lax.broadcasted_iota(jnp.int32, (T, T), 1)
    ltri_t = jnp.where(rt >= ct, 1.0, 0.0).astype(BF16)
    logf = _log_sigmoid(sm_ref[...] + fb_row)
    ctile = _dot_split_rhs(ltri_t, logf, 3) + ccarry_scr[...]
    c_ref[...] = ctile
    ccarry_scr[...] = ctile[T - 1:T, :]

    ri = lax.broadcasted_iota(jnp.int32, (CHUNK, CHUNK), 0)
    ci = lax.broadcasted_iota(jnp.int32, (CHUNK, CHUNK), 1)
    causal = ri >= ci
    ltri = jnp.where(causal, 1.0, 0.0).astype(BF16)
    utri = jnp.where(ri <= ci, 1.0, 0.0).astype(BF16)
    lane = lax.broadcasted_iota(jnp.int32, (CHUNK, LANES), 1)
    first_head = lane < HEADDIM
    e64 = e64_ref[...]

    for ck in range(T // CHUNK):
        r = slice(ck * CHUNK, (ck + 1) * CHUNK)
        dt = _softplus(sm_ref[r, :] + dtb_row)
        cs = _dot_split_rhs(ltri, dt * a_row, 3)
        dtt = _softplus(smt_ref[0:SSD_HEADS, r] + dtb_col)
        cst = _dot_split_lhs(dtt * a_col, utri, 3)
        cs_e = _dot_split_lhs(cs, e64, 3)
        dt_e = _dot_split_lhs(dt, e64, 3)
        ecs_e = jnp.exp(cs_e)
        dte_e = jnp.exp(cs_e[CHUNK - 1:CHUNK, :] - cs_e)

        for g in range(SSD_GROUPS):
            bsl = slice(D_MODEL + g * D_STATE, D_MODEL + (g + 1) * D_STATE)
            csl = slice(D_MODEL + (SSD_GROUPS + g) * D_STATE,
                        D_MODEL + (SSD_GROUPS + g + 1) * D_STATE)
            bg = xc_scr[r, bsl].astype(BF16)
            cg = xc_scr[r, csl].astype(BF16)
            cb_ = _dot_nt(cg, bg)
            ys = []
            for pp in range(2):
                p = 2 * g + pp
                ps = slice(p * LANES, (p + 1) * LANES)
                x = xc_scr[r, ps]
                xd = x * dt_e[:, ps]
                xd_b = xd.astype(BF16)
                yd = []
                for hh in range(2):
                    h = 2 * p + hh
                    colb = jnp.broadcast_to(cs[:, h:h + 1], (CHUNK, CHUNK))
                    rowb = jnp.broadcast_to(cst[h:h + 1, :], (CHUNK, CHUNK))
                    lm = jnp.where(causal, jnp.exp(colb - rowb), 0.0)
                    yd.append(_dot((cb_ * lm).astype(BF16), xd_b))
                ydiag = jnp.where(first_head, yd[0], yd[1])
                hin = h_scr[:, ps]
                yoff = _dot(cg, hin.astype(BF16)) * ecs_e[:, ps]
                snew = _dot_tn(bg, (xd * dte_e[:, ps]).astype(BF16))
                h_scr[:, ps] = hin * ecs_e[CHUNK - 1:CHUNK, ps] + snew
                yp = ydiag + yoff + x * dskip_ref[:, ps]
                yp = yp * _silu(z_ref[p, r, :].astype(F32))
                ys.append(yp)
            ss = (jnp.sum(ys[0] * ys[0], axis=-1, keepdims=True)
                  + jnp.sum(ys[1] * ys[1], axis=-1, keepdims=True))
            scale = lax.rsqrt(ss * (1.0 / (2 * LANES)) + EPS)
            for pp in range(2):
                p = 2 * g + pp
                ps = slice(p * LANES, (p + 1) * LANES)
                y_ref[p, r, :] = (ys[pp] * scale * norm_ref[:, ps]).astype(BF16)


def _ssd(proj, sm, smt, convw, convb, prow, pcol, dskip_e, norm, e64, *, bsz, seq, T):
    m = bsz * seq
    nt = seq // T
    kern = functools.partial(_ssd_kernel, T=T)
    const2 = lambda b, t: (0, 0)
    return pl.pallas_call(
        kern,
        grid=(bsz, nt),
        in_specs=[
            pl.BlockSpec((2 * CB, T, LANES), lambda b, t: (1, b * nt + t, 0)),
            pl.BlockSpec((CB, T, LANES), lambda b, t: (0, b * nt + t, 0)),
            pl.BlockSpec((T, LANES), lambda b, t: (b * nt + t, 0)),
            pl.BlockSpec((LANES, T), lambda b, t: (0, b * nt + t)),
            pl.BlockSpec(convw.shape, const2),
            pl.BlockSpec(convb.shape, const2),
            pl.BlockSpec(prow.shape, const2),
            pl.BlockSpec(pcol.shape, lambda b, t: (0, 0, 0)),
            pl.BlockSpec(dskip_e.shape, const2),
            pl.BlockSpec(norm.shape, const2),
            pl.BlockSpec(e64.shape, const2),
        ],
        out_specs=[
            pl.BlockSpec((CB, T, LANES), lambda b, t: (0, b * nt + t, 0)),
            pl.BlockSpec((T, LANES), lambda b, t: (b * nt + t, 0)),
        ],
        out_shape=[
            jax.ShapeDtypeStruct((CB, m, LANES), BF16),
            jax.ShapeDtypeStruct((m, LANES), F32),
        ],
        scratch_shapes=[
            pltpu.VMEM((D_STATE, D_MODEL), F32),
            pltpu.VMEM((T + 8, 2 * D_MODEL), F32),
            pltpu.VMEM((T, 2 * D_MODEL), F32),
            pltpu.VMEM((1, LANES), F32),
            pltpu.VMEM((SSD_CONV - 1, T, LANES), F32),
        ],
        compiler_params=pltpu.CompilerParams(
            dimension_semantics=("arbitrary", "arbitrary"), vmem_limit_bytes=VMEM_LIMIT),
        name="ssd",
    )(proj, proj, sm, smt, convw, convb, prow, pcol, dskip_e, norm, e64)


def _fox_kernel(q_ref, k_ref, vt_ref, cq_ref, c_ref, o_ref, qh_scr, m_scr, acc_scr, *,
                tq, tk, c_lane0):
    pair = pl.program_id(1)
    i = pl.program_id(2)
    nstrip = tq // STRIP
    lane_k = lax.broadcasted_iota(jnp.int32, (tk, LANES), 1)
    lane = lax.broadcasted_iota(jnp.int32, (tq, LANES), 1)
    q = q_ref[...]
    zero = jnp.zeros_like(q)
    qh_scr[0] = jnp.where(lane < HEADDIM, q, zero)
    qh_scr[1] = jnp.where(lane >= HEADDIM, q, zero)
    m_scr[...] = jnp.full_like(m_scr, NEG_BIG)
    acc_scr[...] = jnp.zeros_like(acc_scr)
    ones = jnp.ones((ONES_ROWS, tk), BF16)

    def scores(hh, st, k):
        return _dot_nt(k, qh_scr[hh, st * STRIP:(st + 1) * STRIP, :])

    def softmax_pv(hh, st, s, ck2, vta, koff, partial):
        cols = slice(st * STRIP, (st + 1) * STRIP)
        t = s - ck2
        if partial:
            kv_pos = koff + lax.broadcasted_iota(jnp.int32, (tk, STRIP), 0)
            q_pos = i * tq + st * STRIP + lax.broadcasted_iota(jnp.int32, (tk, STRIP), 1)
            t = jnp.where(kv_pos <= q_pos, t, NEG_BIG)
        cq2 = cq_ref[hh:hh + 1, cols] * LOG2E
        m_prev = m_scr[hh, :, cols]
        m_new = jnp.maximum(m_prev, jnp.max(t, axis=0, keepdims=True) + cq2)
        alpha = jnp.exp2(m_prev - m_new)
        p = jnp.exp2(t - (m_new - cq2)).astype(BF16)
        acc_scr[hh, :, cols] = acc_scr[hh, :, cols] * alpha + _dot(vta, p)
        m_scr[hh, :, cols] = m_new

    def strips_of(diag):
        items = []
        for hh in range(2):
            for st in range(nstrip):
                partial = False
                if diag is not None:
                    kv_lo, q_lo = diag * tk, st * STRIP
                    if kv_lo > q_lo + STRIP - 1:
                        continue
                    partial = kv_lo + tk - 1 > q_lo
                items.append((hh, st, partial))
        return items

    def block(j, diag):
        koff = pl.multiple_of(j * tk, tk)
        k = k_ref[pl.ds(koff, tk), :]
        cblk = c_ref[pl.ds(koff, tk), :]
        ck2, vta = [], []
        for hh in range(2):
            sel = lane_k == (c_lane0 + 2 * pair + hh)
            ck2.append(jnp.sum(jnp.where(sel, cblk, 0.0), axis=1, keepdims=True) * LOG2E)
            vta.append(jnp.concatenate(
                [vt_ref[hh * HEADDIM:(hh + 1) * HEADDIM, pl.ds(koff, tk)], ones], axis=0))
        items = strips_of(diag)
        pending = [scores(it[0], it[1], k) for it in items[:LOOKAHEAD]]
        for n, (hh, st, partial) in enumerate(items):
            if n + LOOKAHEAD < len(items):
                nxt = items[n + LOOKAHEAD]
                pending.append(scores(nxt[0], nxt[1], k))
            softmax_pv(hh, st, pending.pop(0), ck2[hh], vta[hh], koff, partial)

    nfull = (i * tq) // tk

    def body(j, carry):
        block(j, None)
        return carry

    lax.fori_loop(0, nfull, body, 0)
    for d in range(tq // tk):
        block(nfull + d, d)

    outs = []
    for hh in range(2):
        acc = acc_scr[hh]
        outs.append(acc[0:HEADDIM, :] * (1.0 / acc[HEADDIM:HEADDIM + 1, :]))
    o_ref[...] = jnp.concatenate(outs, axis=0).T.astype(BF16)


def _fox(proj, vt, c8, c, *, bsz, seq, tq, tk, q_cb0, k_cb0, c_lane0):
    m = bsz * seq
    nq = seq // tq
    npair = FOX_HEADS // 2
    kern = functools.partial(_fox_kernel, tq=tq, tk=tk, c_lane0=c_lane0)
    return pl.pallas_call(
        kern,
        grid=(bsz, npair, nq),
        in_specs=[
            pl.BlockSpec((None, tq, LANES), lambda b, p, i: (q_cb0 + p, b * nq + i, 0)),
            pl.BlockSpec((None, seq, LANES), lambda b, p, i: (k_cb0 + p, b, 0)),
            pl.BlockSpec((None, LANES, seq), lambda b, p, i: (p, 0, b)),
            pl.BlockSpec((None, 8, tq), lambda b, p, i: (p, 0, b * nq + i)),
            pl.BlockSpec((seq, LANES), lambda b, p, i: (b, 0)),
        ],
        out_specs=pl.BlockSpec((None, tq, LANES), lambda b, p, i: (p, b * nq + i, 0)),
        out_shape=jax.ShapeDtypeStruct((npair, m, LANES), BF16),
        scratch_shapes=[
            pltpu.VMEM((2, tq, LANES), BF16),
            pltpu.VMEM((2, 1, tq), F32),
            pltpu.VMEM((2, HEADDIM + ONES_ROWS, tq), F32),
        ],
        compiler_params=pltpu.CompilerParams(
            dimension_semantics=("arbitrary", "arbitrary", "arbitrary"),
            vmem_limit_bytes=VMEM_LIMIT),
        name="fox",
    )(proj, proj, vt, c8, c)


def _outproj0_kernel(y_ref, o_ref, z_ref, x_ref, w_ref, g_ref, out_ref):
    y = _cat_blocks(y_ref, 0, CB)
    o = _cat_blocks(o_ref, 0, CB).astype(F32)
    z = _cat_blocks(z_ref, 0, CB).astype(F32)
    og = (o * _silu(z)).astype(BF16)
    acc = _dot(y, w_ref[0:D_MODEL, :]) + _dot(og, w_ref[D_MODEL:2 * D_MODEL, :])
    ms = jnp.mean(acc * acc, axis=-1, keepdims=True)
    out_ref[...] = x_ref[...] + acc * lax.rsqrt(ms + EPS) * g_ref[...]


def _outproj0(y, o, proj, x2, w, g, *, tm):
    m = x2.shape[0]
    return pl.pallas_call(
        _outproj0_kernel,
        grid=(m // tm,),
        in_specs=[
            pl.BlockSpec((CB, tm, LANES), lambda i: (0, i, 0)),
            pl.BlockSpec((CB, tm, LANES), lambda i: (0, i, 0)),
            pl.BlockSpec((CB, tm, LANES), lambda i: (1, i, 0)),
            pl.BlockSpec((tm, D_MODEL), lambda i: (i, 0)),
            pl.BlockSpec(w.shape, lambda i: (0, 0)),
            pl.BlockSpec((1, D_MODEL), lambda i: (0, 0)),
        ],
        out_specs=pl.BlockSpec((tm, D_MODEL), lambda i: (i, 0)),
        out_shape=jax.ShapeDtypeStruct((m, D_MODEL), F32),
        compiler_params=pltpu.CompilerParams(
            dimension_semantics=("arbitrary",), vmem_limit_bytes=VMEM_LIMIT),
        name="outproj0",
    )(y, o, proj, x2, w, g)


def _inproj1_kernel(x_ref, g_ref, w_ref, h_ref, z_ref, xn_scr, *, seg):
    j = pl.program_id(1)

    @pl.when(j == 0)
    def _():
        x = x_ref[...]
        ms = jnp.mean(x * x, axis=-1, keepdims=True)
        xn_scr[...] = (x * lax.rsqrt(ms + EPS) * g_ref[...]).astype(BF16)

    acc = _dot(xn_scr[...], w_ref[...])
    hv = acc[:, 0:seg] * _sigmoid(acc[:, seg:2 * seg])
    for c in range(seg // LANES):
        h_ref[c] = hv[:, c * LANES:(c + 1) * LANES].astype(BF16)
        z_ref[c] = acc[:, 2 * seg + c * LANES:2 * seg + (c + 1) * LANES].astype(BF16)


def _inproj1(x2, g, w, *, tm, seg):
    m = x2.shape[0]
    tn = 3 * seg
    ncol = w.shape[1] // tn
    scb = seg // LANES
    kern = functools.partial(_inproj1_kernel, seg=seg)
    return pl.pallas_call(
        kern,
        grid=(m // tm, ncol),
        in_specs=[
            pl.BlockSpec((tm, D_MODEL), lambda i, j: (i, 0)),
            pl.BlockSpec((1, D_MODEL), lambda i, j: (0, 0)),
            pl.BlockSpec((D_MODEL, tn), lambda i, j: (0, j)),
        ],
        out_specs=[
            pl.BlockSpec((scb, tm, LANES), lambda i, j: (j, i, 0)),
            pl.BlockSpec((scb, tm, LANES), lambda i, j: (j, i, 0)),
        ],
        out_shape=[
            jax.ShapeDtypeStruct((D_CONV // LANES, m, LANES), BF16),
            jax.ShapeDtypeStruct((D_CONV // LANES, m, LANES), BF16),
        ],
        scratch_shapes=[pltpu.VMEM((tm, D_MODEL), BF16)],
        compiler_params=pltpu.CompilerParams(
            dimension_semantics=("arbitrary", "arbitrary"), vmem_limit_bytes=VMEM_LIMIT),
        name="inproj1",
    )(x2, g, w)


def _conv1_kernel(h_ref, z_ref, x_ref, cw_ref, cb_ref, lg_ref, lb_ref, w_ref, g_ref,
                  out_ref, ext_scr, hc_scr, sh_scr, *, T):
    t = pl.program_id(1)
    ncb = D_CONV // LANES
    halo = CONV_HALO

    @pl.when(t == 0)
    def _():
        ext_scr[0:halo, :] = jnp.zeros((halo, D_CONV), F32)

    for c in range(ncb):
        ext_scr[halo:halo + T, c * LANES:(c + 1) * LANES] = h_ref[c].astype(F32)
    span = T + halo - SUBLANES
    for c in range(ncb):
        cs_ = slice(c * LANES, (c + 1) * LANES)
        for r in range(1, SUBLANES):
            sh_scr[r - 1] = ext_scr[r:r + span, cs_]
        acc = cb_ref[:, cs_]
        for k in range(CONV_WIDTH):
            base, r = divmod(halo - (CONV_WIDTH - 1) + k, SUBLANES)
            base *= SUBLANES
            if r == 0:
                src = ext_scr[base:base + T, cs_]
            else:
                src = sh_scr[r - 1, base:base + T, :]
            acc = acc + src * cw_ref[k:k + 1, cs_]
        hc_scr[:, cs_] = acc
    ext_scr[0:halo, :] = ext_scr[T:T + halo, :]

    hc = hc_scr[...]
    mu = jnp.mean(hc, axis=-1, keepdims=True)
    xc = hc - mu
    var = jnp.mean(xc * xc, axis=-1, keepdims=True)
    hn = xc * lax.rsqrt(var + EPS) * lg_ref[...] + lb_ref[...]
    z = _cat_blocks(z_ref, 0, ncb).astype(F32)
    hg = (_silu(hn) * _silu(z)).astype(BF16)
    acc = _dot(hg, w_ref[...])
    ms = jnp.mean(acc * acc, axis=-1, keepdims=True)
    out_ref[...] = x_ref[...] + acc * lax.rsqrt(ms + EPS) * g_ref[...]


def _conv1(h, z, x2, cw, cb, lg, lb, w, g, *, bsz, seq, T):
    m = bsz * seq
    nt = seq // T
    ncb = D_CONV // LANES
    kern = functools.partial(_conv1_kernel, T=T)
    const2 = lambda b, t: (0, 0)
    return pl.pallas_call(
        kern,
        grid=(bsz, nt),
        in_specs=[
            pl.BlockSpec((ncb, T, LANES), lambda b, t: (0, b * nt + t, 0)),
            pl.BlockSpec((ncb, T, LANES), lambda b, t: (0, b * nt + t, 0)),
            pl.BlockSpec((T, D_MODEL), lambda b, t: (b * nt + t, 0)),
            pl.BlockSpec(cw.shape, const2),
            pl.BlockSpec(cb.shape, const2),
            pl.BlockSpec(lg.shape, const2),
            pl.BlockSpec(lb.shape, const2),
            pl.BlockSpec(w.shape, const2),
            pl.BlockSpec(g.shape, const2),
        ],
        out_specs=pl.BlockSpec((T, D_MODEL), lambda b, t: (b * nt + t, 0)),
        out_shape=jax.ShapeDtypeStruct((m, D_MODEL), F32),
        scratch_shapes=[
            pltpu.VMEM((T + CONV_HALO, D_CONV), F32),
            pltpu.VMEM((T, D_CONV), F32),
            pltpu.VMEM((SUBLANES - 1, T + CONV_HALO - SUBLANES, LANES), F32),
        ],
        compiler_params=pltpu.CompilerParams(
            dimension_semantics=("arbitrary", "arbitrary"), vmem_limit_bytes=VMEM_LIMIT),
        name="conv1",
    )(h, z, x2, cw, cb, lg, lb, w, g)


def _pad_lanes(v, width=LANES):
    return jnp.pad(v, (0, width - v.shape[0]))


def _even_layer(x2, bsz, seq, g_pre, w_in, conv_w, conv_b, dt_bias, a_log, d_skip, fgate_b,
                ssd_norm, w_out, g_post):
    d_xbc = D_MODEL + 2 * SSD_GROUPS * D_STATE
    o_z, o_xbc = 0, 2 * D_MODEL
    o_dt = o_xbc + d_xbc
    o_q = o_dt + SSD_HEADS
    o_k = o_q + D_MODEL
    o_v = o_k + D_MODEL
    o_f = o_v + D_MODEL
    q_scale = HEADDIM ** -0.5 * LOG2E
    w_main = jnp.concatenate(
        [w_in[:, o_z:o_dt], w_in[:, o_q:o_k] * q_scale, w_in[:, o_k:o_v]],
        axis=1).astype(BF16)
    wvt = w_in[:, o_v:o_f].T.astype(BF16)
    w_small = jnp.concatenate(
        [w_in[:, o_dt:o_q], w_in[:, o_f:o_f + FOX_HEADS],
         jnp.zeros((D_MODEL, LANES - SSD_HEADS - FOX_HEADS), F32)], axis=1)
    wsh = w_small.astype(BF16)
    wsl = (w_small - wsh.astype(F32)).astype(BF16)

    proj, vt, sm, smt = _inproj0(x2, g_pre[None, :], w_main, wvt, wsh, wsl, tm=1024, tn=1024)

    zeros16 = jnp.zeros((SSD_HEADS,), F32)
    prow = jnp.stack([
        _pad_lanes(dt_bias),
        _pad_lanes(a_log),
        _pad_lanes(jnp.concatenate([zeros16, fgate_b])),
    ] + [jnp.zeros((LANES,), F32)] * 5)
    pcol = jnp.stack([jnp.broadcast_to(dt_bias[:, None], (SSD_HEADS, LANES)),
                      jnp.broadcast_to(a_log[:, None], (SSD_HEADS, LANES))])
    dskip_e = jnp.repeat(d_skip, HEADDIM)[None, :]
    head_of_lane = jnp.arange(D_MODEL) // HEADDIM
    e64 = (jnp.arange(LANES)[:, None] == head_of_lane[None, :]).astype(BF16)

    y, c = _ssd(proj, sm, smt, conv_w, conv_b[None, :], prow, pcol, dskip_e,
                ssd_norm[None, :], e64, bsz=bsz, seq=seq, T=256)

    ct = c[:, SSD_HEADS:SSD_HEADS + FOX_HEADS].T.reshape(FOX_HEADS // 2, 2, -1)
    c8 = jnp.pad(ct, ((0, 0), (0, 6), (0, 0)))

    o = _fox(proj, vt, c8, c, bsz=bsz, seq=seq, tq=1024, tk=512,
             q_cb0=4 * CB, k_cb0=5 * CB, c_lane0=SSD_HEADS)
    return _outproj0(y, o, proj, x2, w_out.astype(BF16), g_post[None, :], tm=512)


def _odd_layer(x2, bsz, seq, g_pre, w_in, conv_w, conv_b, ln_g, ln_b, w_out, g_post):
    seg = 512
    cols = []
    for j in range(D_CONV // seg):
        for part in range(3):
            cols.append(w_in[:, part * D_CONV + j * seg:part * D_CONV + (j + 1) * seg])
    w = jnp.concatenate(cols, axis=1).astype(BF16)
    h, z = _inproj1(x2, g_pre[None, :], w, tm=1024, seg=seg)
    return _conv1(h, z, x2, conv_w, conv_b[None, :], ln_g[None, :], ln_b[None, :],
                  w_out.astype(BF16), g_post[None, :], bsz=bsz, seq=seq, T=256)


def kernel(x, e_norm_pre, e_w_in, e_conv_w, e_conv_b, e_dt_bias, e_a_log, e_d_skip, e_fgate_b,
           e_ssd_norm, e_w_out, e_norm_post, o_norm_pre, o_w_in, o_conv_w, o_conv_b, o_ln_g,
           o_ln_b, o_w_out, o_norm_post):
    bsz, seq, d = x.shape
    x2 = x.reshape(bsz * seq, d)
    depth = e_w_in.shape[0] + o_w_in.shape[0]
    for layer in range(depth):
        i = layer // 2
        if layer % 2 == 0:
            x2 = _even_layer(x2, bsz, seq, e_norm_pre[i], e_w_in[i], e_conv_w[i], e_conv_b[i],
                             e_dt_bias[i], e_a_log[i], e_d_skip[i], e_fgate_b[i],
                             e_ssd_norm[i], e_w_out[i], e_norm_post[i])
        else:
            x2 = _odd_layer(x2, bsz, seq, o_norm_pre[i], o_w_in[i], o_conv_w[i], o_conv_b[i],
                            o_ln_g[i], o_ln_b[i], o_w_out[i], o_norm_post[i])
    return x2.reshape(bsz, seq, d)
```

```python
import functools
import math

import jax
import jax.numpy as jnp
from jax import lax
from jax.experimental import pallas as pl
from jax.experimental.pallas import tpu as pltpu

F32 = jnp.float32
BF16 = jnp.bfloat16

LANES = 128
SUBLANES = 8
D_MODEL = 1024
SSD_HEADS = 16
SSD_GROUPS = 4
HEADDIM = 64
D_STATE = 128
CHUNK = 128
SSD_CONV = 4
FOX_HEADS = 16
D_CONV = 2 * D_MODEL
CONV_WIDTH = 31
EPS = 1e-6
NEG_BIG = -1e30
LOG2E = 1.4426950408889634
STRIP = 256
ONES_ROWS = 16
CONV_HALO = 32
LOOKAHEAD = 4
FOX_UNROLL = 2
VMEM_LIMIT = 56 * 1024 * 1024

CB = D_MODEL // LANES


def _dot(a, b):
    return jnp.dot(a, b, preferred_element_type=F32)


def _dot_nt(a, b):
    return lax.dot_general(a, b, (((1,), (1,)), ((), ())), preferred_element_type=F32)


def _dot_tn(a, b):
    return lax.dot_general(a, b, (((0,), (0,)), ((), ())), preferred_element_type=F32)


def _split(a, n):
    pieces = []
    r = a
    for _ in range(n):
        p = r.astype(BF16)
        pieces.append(p)
        r = r - p.astype(F32)
    return pieces


def _dot_split_lhs(a, m_bf16, n):
    out = None
    for p in _split(a, n):
        t = _dot(p, m_bf16)
        out = t if out is None else out + t
    return out


def _dot_split_rhs(m_bf16, a, n):
    out = None
    for p in _split(a, n):
        t = _dot(m_bf16, p)
        out = t if out is None else out + t
    return out


def _sigmoid(x):
    return 1.0 / (1.0 + jnp.exp(-x))


def _silu(x):
    return x * _sigmoid(x)


def _softplus(x):
    return jnp.maximum(x, 0.0) + jnp.log(1.0 + jnp.exp(-jnp.abs(x)))


def _log_sigmoid(x):
    return jnp.minimum(x, 0.0) - jnp.log(1.0 + jnp.exp(-jnp.abs(x)))


def _cat_blocks(ref, lo, hi, rows=None):
    if rows is None:
        return jnp.concatenate([ref[c] for c in range(lo, hi)], axis=-1)
    return jnp.concatenate([ref[c, rows, :] for c in range(lo, hi)], axis=-1)


def _inproj0_kernel(x_ref, g_ref, w_ref, wvt_ref, wsh_ref, wsl_ref,
                    proj_ref, vt_ref, sm_ref, smt_ref, xn_scr, *, ncb):
    j = pl.program_id(1)

    @pl.when(j == 0)
    def _():
        x = x_ref[...]
        ms = jnp.mean(x * x, axis=-1, keepdims=True)
        xn = x * lax.rsqrt(ms + EPS) * g_ref[...]
        hi = xn.astype(BF16)
        lo = (xn - hi.astype(F32)).astype(BF16)
        xn_scr[...] = hi
        small = _dot(hi, wsh_ref[...]) + _dot(lo, wsh_ref[...]) + _dot(hi, wsl_ref[...])
        sm_ref[...] = small
        smt_ref[...] = small.T
        vt = _dot_nt(wvt_ref[...], hi)
        for c in range(CB):
            vt_ref[c] = vt[c * LANES:(c + 1) * LANES, :].astype(BF16)

    acc = _dot(xn_scr[...], w_ref[...])
    for c in range(ncb):
        proj_ref[c] = acc[:, c * LANES:(c + 1) * LANES].astype(BF16)


def _inproj0(x2, g, w_main, wvt, wsh, wsl, *, tm, tn):
    m = x2.shape[0]
    ncol = w_main.shape[1] // tn
    ncb = tn // LANES
    kern = functools.partial(_inproj0_kernel, ncb=ncb)
    return pl.pallas_call(
        kern,
        grid=(m // tm, ncol),
        in_specs=[
            pl.BlockSpec((tm, D_MODEL), lambda i, j: (i, 0)),
            pl.BlockSpec((1, D_MODEL), lambda i, j: (0, 0)),
            pl.BlockSpec((D_MODEL, tn), lambda i, j: (0, j)),
            pl.BlockSpec((D_MODEL, D_MODEL), lambda i, j: (0, 0)),
            pl.BlockSpec((D_MODEL, LANES), lambda i, j: (0, 0)),
            pl.BlockSpec((D_MODEL, LANES), lambda i, j: (0, 0)),
        ],
        out_specs=[
            pl.BlockSpec((ncb, tm, LANES), lambda i, j: (j, i, 0)),
            pl.BlockSpec((CB, LANES, tm), lambda i, j: (0, 0, i)),
            pl.BlockSpec((tm, LANES), lambda i, j: (i, 0)),
            pl.BlockSpec((LANES, tm), lambda i, j: (0, i)),
        ],
        out_shape=[
            jax.ShapeDtypeStruct((w_main.shape[1] // LANES, m, LANES), BF16),
            jax.ShapeDtypeStruct((CB, LANES, m), BF16),
            jax.ShapeDtypeStruct((m, LANES), F32),
            jax.ShapeDtypeStruct((LANES, m), F32),
        ],
        scratch_shapes=[pltpu.VMEM((tm, D_MODEL), BF16)],
        compiler_params=pltpu.CompilerParams(
            dimension_semantics=("arbitrary", "arbitrary"), vmem_limit_bytes=VMEM_LIMIT),
        name="inproj0",
    )(x2, g, w_main, wvt, wsh, wsl)


def _ssd_kernel(xbc_ref, z_ref, sm_ref, smt_ref, convw_ref, convb_ref, prow_ref, pcol_ref,
                dskip_ref, norm_ref, e64_ref,
                y_ref, c_ref,
                h_scr, ext_scr, xc_scr, ccarry_scr, sh_scr, *, T):
    t = pl.program_id(1)
    nxc = 2 * CB

    @pl.when(t == 0)
    def _():
        h_scr[...] = jnp.zeros_like(h_scr)
        ext_scr[0:8, :] = jnp.zeros((8, nxc * LANES), F32)
        ccarry_scr[...] = jnp.zeros_like(ccarry_scr)

    for c in range(nxc):
        ext_scr[8:8 + T, c * LANES:(c + 1) * LANES] = xbc_ref[c].astype(F32)
    for c in range(nxc):
        cs_ = slice(c * LANES, (c + 1) * LANES)
        for k in range(SSD_CONV - 1):
            off = SUBLANES - (SSD_CONV - 1) + k
            sh_scr[k] = ext_scr[off:off + T, cs_]
        acc = convb_ref[:, cs_] + (ext_scr[SUBLANES:SUBLANES + T, cs_]
                                   * convw_ref[SSD_CONV - 1:SSD_CONV, cs_])
        for k in range(SSD_CONV - 1):
            acc = acc + sh_scr[k] * convw_ref[k:k + 1, cs_]
        xc_scr[:, cs_] = _silu(acc)
    ext_scr[0:8, :] = ext_scr[T:T + 8, :]

    dtb_row = prow_ref[0:1, :]
    a_row = -jnp.exp(prow_ref[1:2, :])
    fb_row = prow_ref[2:3, :]
    dtb_col = pcol_ref[0]
    a_col = -jnp.exp(pcol_ref[1])

    rt = lax.broadcasted_iota(jnp.int32, (T, T), 0)
    ct = lax.broadcasted_iota(jnp.int32, (T, T), 1)
    ltri_t = jnp.where(rt >= ct, 1.0, 0.0).astype(BF16)
    logf = _log_sigmoid(sm_ref[...] + fb_row)
    ctile = _dot_split_rhs(ltri_t, logf, 3) + ccarry_scr[...]
    c_ref[...] = ctile
    ccarry_scr[...] = ctile[T - 1:T, :]

    ri = lax.broadcasted_iota(jnp.int32, (CHUNK, CHUNK), 0)
    ci = lax.broadcasted_iota(jnp.int32, (CHUNK, CHUNK), 1)
    causal = ri >= ci
    ltri = jnp.where(causal, 1.0, 0.0).astype(BF16)
    utri = jnp.where(ri <= ci, 1.0, 0.0).astype(BF16)
    lane = lax.broadcasted_iota(jnp.int32, (CHUNK, LANES), 1)
    first_head = lane < HEADDIM
    e64 = e64_ref[...]

    for ck in range(T // CHUNK):
        r = slice(ck * CHUNK, (ck + 1) * CHUNK)
        dt = _softplus(sm_ref[r, :] + dtb_row)
        cs = _dot_split_rhs(ltri, dt * a_row, 3)
        dtt = _softplus(smt_ref[0:SSD_HEADS, r] + dtb_col)
        cst = _dot_split_lhs(dtt * a_col, utri, 3)
        cs_e = _dot_split_lhs(cs, e64, 3)
        dt_e = _dot_split_lhs(dt, e64, 3)
        ecs_e = jnp.exp(cs_e)
        dte_e = jnp.exp(cs_e[CHUNK - 1:CHUNK, :] - cs_e)

        for g in range(SSD_GROUPS):
            bsl = slice(D_MODEL + g * D_STATE, D_MODEL + (g + 1) * D_STATE)
            csl = slice(D_MODEL + (SSD_GROUPS + g) * D_STATE,
                        D_MODEL + (SSD_GROUPS + g + 1) * D_STATE)
            bg = xc_scr[r, bsl].astype(BF16)
            cg = xc_scr[r, csl].astype(BF16)
            cb_ = _dot_nt(cg, bg)
            ys = []
            for pp in range(2):
                p = 2 * g + pp
                ps = slice(p * LANES, (p + 1) * LANES)
                x = xc_scr[r, ps]
                xd = x * dt_e[:, ps]
                xd_b = xd.astype(BF16)
                yd = []
                for hh in range(2):
                    h = 2 * p + hh
                    colb = jnp.broadcast_to(cs[:, h:h + 1], (CHUNK, CHUNK))
                    rowb = jnp.broadcast_to(cst[h:h + 1, :], (CHUNK, CHUNK))
                    lm = jnp.where(causal, jnp.exp(colb - rowb), 0.0)
                    yd.append(_dot((cb_ * lm).astype(BF16), xd_b))
                ydiag = jnp.where(first_head, yd[0], yd[1])
                hin = h_scr[:, ps]
                yoff = _dot(cg, hin.astype(BF16)) * ecs_e[:, ps]
                snew = _dot_tn(bg, (xd * dte_e[:, ps]).astype(BF16))
                h_scr[:, ps] = hin * ecs_e[CHUNK - 1:CHUNK, ps] + snew
                yp = ydiag + yoff + x * dskip_ref[:, ps]
                yp = yp * _silu(z_ref[p, r, :].astype(F32))
                ys.append(yp)
            ss = (jnp.sum(ys[0] * ys[0], axis=-1, keepdims=True)
                  + jnp.sum(ys[1] * ys[1], axis=-1, keepdims=True))
            scale = lax.rsqrt(ss * (1.0 / (2 * LANES)) + EPS)
            for pp in range(2):
                p = 2 * g + pp
                ps = slice(p * LANES, (p + 1) * LANES)
                y_ref[p, r, :] = (ys[pp] * scale * norm_ref[:, ps]).astype(BF16)


def _ssd(proj, sm, smt, convw, convb, prow, pcol, dskip_e, norm, e64, *, bsz, seq, T):
    m = bsz * seq
    nt = seq // T
    kern = functools.partial(_ssd_kernel, T=T)
    const2 = lambda b, t: (0, 0)
    return pl.pallas_call(
        kern,
        grid=(bsz, nt),
        in_specs=[
            pl.BlockSpec((2 * CB, T, LANES), lambda b, t: (1, b * nt + t, 0)),
            pl.BlockSpec((CB, T, LANES), lambda b, t: (0, b * nt + t, 0)),
            pl.BlockSpec((T, LANES), lambda b, t: (b * nt + t, 0)),
            pl.BlockSpec((LANES, T), lambda b, t: (0, b * nt + t)),
            pl.BlockSpec(convw.shape, const2),
            pl.BlockSpec(convb.shape, const2),
            pl.BlockSpec(prow.shape, const2),
            pl.BlockSpec(pcol.shape, lambda b, t: (0, 0, 0)),
            pl.BlockSpec(dskip_e.shape, const2),
            pl.BlockSpec(norm.shape, const2),
            pl.BlockSpec(e64.shape, const2),
        ],
        out_specs=[
            pl.BlockSpec((CB, T, LANES), lambda b, t: (0, b * nt + t, 0)),
            pl.BlockSpec((T, LANES), lambda b, t: (b * nt + t, 0)),
        ],
        out_shape=[
            jax.ShapeDtypeStruct((CB, m, LANES), BF16),
            jax.ShapeDtypeStruct((m, LANES), F32),
        ],
        scratch_shapes=[
            pltpu.VMEM((D_STATE, D_MODEL), F32),
            pltpu.VMEM((T + 8, 2 * D_MODEL), F32),
            pltpu.VMEM((T, 2 * D_MODEL), F32),
            pltpu.VMEM((1, LANES), F32),
            pltpu.VMEM((SSD_CONV - 1, T, LANES), F32),
        ],
        compiler_params=pltpu.CompilerParams(
            dimension_semantics=("arbitrary", "arbitrary"), vmem_limit_bytes=VMEM_LIMIT),
        name="ssd",
    )(proj, proj, sm, smt, convw, convb, prow, pcol, dskip_e, norm, e64)


def _fox_kernel(q_ref, k_ref, vt_ref, cq_ref, c_ref, o_ref, qh_scr, m_scr, acc_scr, *,
                tq, tk, c_lane0):
    pair = pl.program_id(1)
    i = pl.program_id(2)
    nstrip = tq // STRIP
    lane_k = lax.broadcasted_iota(jnp.int32, (tk, LANES), 1)
    lane = lax.broadcasted_iota(jnp.int32, (tq, LANES), 1)
    q = q_ref[...]
    zero = jnp.zeros_like(q)
    qh_scr[0] = jnp.where(lane < HEADDIM, q, zero)
    qh_scr[1] = jnp.where(lane >= HEADDIM, q, zero)
    m_scr[...] = jnp.full_like(m_scr, NEG_BIG)
    acc_scr[...] = jnp.zeros_like(acc_scr)
    ones = jnp.ones((ONES_ROWS, tk), BF16)

    def scores(hh, st, k):
        return _dot_nt(k, qh_scr[hh, st * STRIP:(st + 1) * STRIP, :])

    def softmax_pv(hh, st, s, ck2, vta, koff, partial):
        cols = slice(st * STRIP, (st + 1) * STRIP)
        t = s - ck2
        if partial:
            kv_pos = koff + lax.broadcasted_iota(jnp.int32, (tk, STRIP), 0)
            q_pos = i * tq + st * STRIP + lax.broadcasted_iota(jnp.int32, (tk, STRIP), 1)
            t = jnp.where(kv_pos <= q_pos, t, NEG_BIG)
        cq2 = cq_ref[hh:hh + 1, cols] * LOG2E
        m_prev = m_scr[hh, :, cols]
        m_new = jnp.maximum(m_prev, jnp.max(t, axis=0, keepdims=True) + cq2)
        alpha = jnp.exp2(m_prev - m_new)
        p = jnp.exp2(t - (m_new - cq2)).astype(BF16)
        acc_scr[hh, :, cols] = acc_scr[hh, :, cols] * alpha + _dot(vta, p)
        m_scr[hh, :, cols] = m_new

    def strips_of(diag):
        items = []
        for hh in range(2):
            for st in range(nstrip):
                partial = False
                if diag is not None:
                    kv_lo, q_lo = diag * tk, st * STRIP
                    if kv_lo > q_lo + STRIP - 1:
                        continue
                    partial = kv_lo + tk - 1 > q_lo
                items.append((hh, st, partial))
        return items

    def blocks(j0, diags):
        koffs, ks, ck2s, vtas, items = [], [], [], [], []
        for b, diag in enumerate(diags):
            koff = pl.multiple_of((j0 + b) * tk, tk)
            cblk = c_ref[pl.ds(koff, tk), :]
            ck2, vta = [], []
            for hh in range(2):
                sel = lane_k == (c_lane0 + 2 * pair + hh)
                ck2.append(jnp.sum(jnp.where(sel, cblk, 0.0), axis=1, keepdims=True) * LOG2E)
                vta.append(jnp.concatenate(
                    [vt_ref[hh * HEADDIM:(hh + 1) * HEADDIM, pl.ds(koff, tk)], ones], axis=0))
            koffs.append(koff)
            ks.append(k_ref[pl.ds(koff, tk), :])
            ck2s.append(ck2)
            vtas.append(vta)
            items += [(b,) + it for it in strips_of(diag)]
        pending = [scores(hh, st, ks[b]) for b, hh, st, _ in items[:LOOKAHEAD]]
        for n, (b, hh, st, partial) in enumerate(items):
            if n + LOOKAHEAD < len(items):
                nb, nhh, nst, _ = items[n + LOOKAHEAD]
                pending.append(scores(nhh, nst, ks[nb]))
            softmax_pv(hh, st, pending.pop(0), ck2s[b][hh], vtas[b][hh], koffs[b], partial)

    ndiag = tq // tk
    nfull = i * ndiag
    unroll = math.gcd(ndiag, FOX_UNROLL)

    def body(j, carry):
        blocks(j * unroll, [None] * unroll)
        return carry

    lax.fori_loop(0, nfull // unroll, body, 0)
    blocks(nfull, list(range(ndiag)))

    outs = []
    for hh in range(2):
        acc = acc_scr[hh]
        outs.append(acc[0:HEADDIM, :] * (1.0 / acc[HEADDIM:HEADDIM + 1, :]))
    o_ref[...] = jnp.concatenate(outs, axis=0).T.astype(BF16)


def _fox(proj, vt, c8, c, *, bsz, seq, tq, tk, q_cb0, k_cb0, c_lane0):
    m = bsz * seq
    nq = seq // tq
    npair = FOX_HEADS // 2
    kern = functools.partial(_fox_kernel, tq=tq, tk=tk, c_lane0=c_lane0)
    return pl.pallas_call(
        kern,
        grid=(bsz, npair, nq),
        in_specs=[
            pl.BlockSpec((None, tq, LANES), lambda b, p, i: (q_cb0 + p, b * nq + i, 0)),
            pl.BlockSpec((None, seq, LANES), lambda b, p, i: (k_cb0 + p, b, 0)),
            pl.BlockSpec((None, LANES, seq), lambda b, p, i: (p, 0, b)),
            pl.BlockSpec((None, 8, tq), lambda b, p, i: (p, 0, b * nq + i)),
            pl.BlockSpec((seq, LANES), lambda b, p, i: (b, 0)),
        ],
        out_specs=pl.BlockSpec((None, tq, LANES), lambda b, p, i: (p, b * nq + i, 0)),
        out_shape=jax.ShapeDtypeStruct((npair, m, LANES), BF16),
        scratch_shapes=[
            pltpu.VMEM((2, tq, LANES), BF16),
            pltpu.VMEM((2, 1, tq), F32),
            pltpu.VMEM((2, HEADDIM + ONES_ROWS, tq), F32),
        ],
        compiler_params=pltpu.CompilerParams(
            dimension_semantics=("arbitrary", "arbitrary", "arbitrary"),
            vmem_limit_bytes=VMEM_LIMIT),
        name="fox",
    )(proj, proj, vt, c8, c)


def _outproj0_kernel(y_ref, o_ref, z_ref, x_ref, w_ref, g_ref, out_ref):
    y = _cat_blocks(y_ref, 0, CB)
    o = _cat_blocks(o_ref, 0, CB).astype(F32)
    z = _cat_blocks(z_ref, 0, CB).astype(F32)
    og = (o * _silu(z)).astype(BF16)
    acc = _dot(y, w_ref[0:D_MODEL, :]) + _dot(og, w_ref[D_MODEL:2 * D_MODEL, :])
    ms = jnp.mean(acc * acc, axis=-1, keepdims=True)
    out_ref[...] = x_ref[...] + acc * lax.rsqrt(ms + EPS) * g_ref[...]


def _outproj0(y, o, proj, x2, w, g, *, tm):
    m = x2.shape[0]
    return pl.pallas_call(
        _outproj0_kernel,
        grid=(m // tm,),
        in_specs=[
            pl.BlockSpec((CB, tm, LANES), lambda i: (0, i, 0)),
            pl.BlockSpec((CB, tm, LANES), lambda i: (0, i, 0)),
            pl.BlockSpec((CB, tm, LANES), lambda i: (1, i, 0)),
            pl.BlockSpec((tm, D_MODEL), lambda i: (i, 0)),
            pl.BlockSpec(w.shape, lambda i: (0, 0)),
            pl.BlockSpec((1, D_MODEL), lambda i: (0, 0)),
        ],
        out_specs=pl.BlockSpec((tm, D_MODEL), lambda i: (i, 0)),
        out_shape=jax.ShapeDtypeStruct((m, D_MODEL), F32),
        compiler_params=pltpu.CompilerParams(
            dimension_semantics=("arbitrary",), vmem_limit_bytes=VMEM_LIMIT),
        name="outproj0",
    )(y, o, proj, x2, w, g)


def _inproj1_kernel(x_ref, g_ref, w_ref, h_ref, z_ref, xn_scr, *, seg):
    j = pl.program_id(1)

    @pl.when(j == 0)
    def _():
        x = x_ref[...]
        ms = jnp.mean(x * x, axis=-1, keepdims=True)
        xn_scr[...] = (x * lax.rsqrt(ms + EPS) * g_ref[...]).astype(BF16)

    acc = _dot(xn_scr[...], w_ref[...])
    hv = acc[:, 0:seg] * _sigmoid(acc[:, seg:2 * seg])
    for c in range(seg // LANES):
        h_ref[c] = hv[:, c * LANES:(c + 1) * LANES].astype(BF16)
        z_ref[c] = acc[:, 2 * seg + c * LANES:2 * seg + (c + 1) * LANES].astype(BF16)


def _inproj1(x2, g, w, *, tm, seg):
    m = x2.shape[0]
    tn = 3 * seg
    ncol = w.shape[1] // tn
    scb = seg // LANES
    kern = functools.partial(_inproj1_kernel, seg=seg)
    return pl.pallas_call(
        kern,
        grid=(m // tm, ncol),
        in_specs=[
            pl.BlockSpec((tm, D_MODEL), lambda i, j: (i, 0)),
            pl.BlockSpec((1, D_MODEL), lambda i, j: (0, 0)),
            pl.BlockSpec((D_MODEL, tn), lambda i, j: (0, j)),
        ],
        out_specs=[
            pl.BlockSpec((scb, tm, LANES), lambda i, j: (j, i, 0)),
            pl.BlockSpec((scb, tm, LANES), lambda i, j: (j, i, 0)),
        ],
        out_shape=[
            jax.ShapeDtypeStruct((D_CONV // LANES, m, LANES), BF16),
            jax.ShapeDtypeStruct((D_CONV // LANES, m, LANES), BF16),
        ],
        scratch_shapes=[pltpu.VMEM((tm, D_MODEL), BF16)],
        compiler_params=pltpu.CompilerParams(
            dimension_semantics=("arbitrary", "arbitrary"), vmem_limit_bytes=VMEM_LIMIT),
        name="inproj1",
    )(x2, g, w)


def _conv1_kernel(h_ref, z_ref, x_ref, cw_ref, cb_ref, lg_ref, lb_ref, w_ref, g_ref,
                  out_ref, ext_scr, hc_scr, sh_scr, *, T):
    t = pl.program_id(1)
    ncb = D_CONV // LANES
    halo = CONV_HALO

    @pl.when(t == 0)
    def _():
        ext_scr[0:halo, :] = jnp.zeros((halo, D_CONV), F32)

    for c in range(ncb):
        ext_scr[halo:halo + T, c * LANES:(c + 1) * LANES] = h_ref[c].astype(F32)
    span = T + halo - SUBLANES
    for c in range(ncb):
        cs_ = slice(c * LANES, (c + 1) * LANES)
        for r in range(1, SUBLANES):
            sh_scr[r - 1] = ext_scr[r:r + span, cs_]
        acc = cb_ref[:, cs_]
        for k in range(CONV_WIDTH):
            base, r = divmod(halo - (CONV_WIDTH - 1) + k, SUBLANES)
            base *= SUBLANES
            if r == 0:
                src = ext_scr[base:base + T, cs_]
            else:
                src = sh_scr[r - 1, base:base + T, :]
            acc = acc + src * cw_ref[k:k + 1, cs_]
        hc_scr[:, cs_] = acc
    ext_scr[0:halo, :] = ext_scr[T:T + halo, :]

    hc = hc_scr[...]
    mu = jnp.mean(hc, axis=-1, keepdims=True)
    xc = hc - mu
    var = jnp.mean(xc * xc, axis=-1, keepdims=True)
    hn = xc * lax.rsqrt(var + EPS) * lg_ref[...] + lb_ref[...]
    z = _cat_blocks(z_ref, 0, ncb).astype(F32)
    hg = (_silu(hn) * _silu(z)).astype(BF16)
    acc = _dot(hg, w_ref[...])
    ms = jnp.mean(acc * acc, axis=-1, keepdims=True)
    out_ref[...] = x_ref[...] + acc * lax.rsqrt(ms + EPS) * g_ref[...]


def _conv1(h, z, x2, cw, cb, lg, lb, w, g, *, bsz, seq, T):
    m = bsz * seq
    nt = seq // T
    ncb = D_CONV // LANES
    kern = functools.partial(_conv1_kernel, T=T)
    const2 = lambda b, t: (0, 0)
    return pl.pallas_call(
        kern,
        grid=(bsz, nt),
        in_specs=[
            pl.BlockSpec((ncb, T, LANES), lambda b, t: (0, b * nt + t, 0)),
            pl.BlockSpec((ncb, T, LANES), lambda b, t: (0, b * nt + t, 0)),
            pl.BlockSpec((T, D_MODEL), lambda b, t: (b * nt + t, 0)),
            pl.BlockSpec(cw.shape, const2),
            pl.BlockSpec(cb.shape, const2),
            pl.BlockSpec(lg.shape, const2),
            pl.BlockSpec(lb.shape, const2),
            pl.BlockSpec(w.shape, const2),
            pl.BlockSpec(g.shape, const2),
        ],
        out_specs=pl.BlockSpec((T, D_MODEL), lambda b, t: (b * nt + t, 0)),
        out_shape=jax.ShapeDtypeStruct((m, D_MODEL), F32),
        scratch_shapes=[
            pltpu.VMEM((T + CONV_HALO, D_CONV), F32),
            pltpu.VMEM((T, D_CONV), F32),
            pltpu.VMEM((SUBLANES - 1, T + CONV_HALO - SUBLANES, LANES), F32),
        ],
        compiler_params=pltpu.CompilerParams(
            dimension_semantics=("arbitrary", "arbitrary"), vmem_limit_bytes=VMEM_LIMIT),
        name="conv1",
    )(h, z, x2, cw, cb, lg, lb, w, g)


def _pad_lanes(v, width=LANES):
    return jnp.pad(v, (0, width - v.shape[0]))


def _even_layer(x2, bsz, seq, g_pre, w_in, conv_w, conv_b, dt_bias, a_log, d_skip, fgate_b,
                ssd_norm, w_out, g_post):
    d_xbc = D_MODEL + 2 * SSD_GROUPS * D_STATE
    o_z, o_xbc = 0, 2 * D_MODEL
    o_dt = o_xbc + d_xbc
    o_q = o_dt + SSD_HEADS
    o_k = o_q + D_MODEL
    o_v = o_k + D_MODEL
    o_f = o_v + D_MODEL
    q_scale = HEADDIM ** -0.5 * LOG2E
    w_main = jnp.concatenate(
        [w_in[:, o_z:o_dt], w_in[:, o_q:o_k] * q_scale, w_in[:, o_k:o_v]],
        axis=1).astype(BF16)
    wvt = w_in[:, o_v:o_f].T.astype(BF16)
    w_small = jnp.concatenate(
        [w_in[:, o_dt:o_q], w_in[:, o_f:o_f + FOX_HEADS],
         jnp.zeros((D_MODEL, LANES - SSD_HEADS - FOX_HEADS), F32)], axis=1)
    wsh = w_small.astype(BF16)
    wsl = (w_small - wsh.astype(F32)).astype(BF16)

    proj, vt, sm, smt = _inproj0(x2, g_pre[None, :], w_main, wvt, wsh, wsl, tm=1024, tn=1024)

    zeros16 = jnp.zeros((SSD_HEADS,), F32)
    prow = jnp.stack([
        _pad_lanes(dt_bias),
        _pad_lanes(a_log),
        _pad_lanes(jnp.concatenate([zeros16, fgate_b])),
    ] + [jnp.zeros((LANES,), F32)] * 5)
    pcol = jnp.stack([jnp.broadcast_to(dt_bias[:, None], (SSD_HEADS, LANES)),
                      jnp.broadcast_to(a_log[:, None], (SSD_HEADS, LANES))])
    dskip_e = jnp.repeat(d_skip, HEADDIM)[None, :]
    head_of_lane = jnp.arange(D_MODEL) // HEADDIM
    e64 = (jnp.arange(LANES)[:, None] == head_of_lane[None, :]).astype(BF16)

    y, c = _ssd(proj, sm, smt, conv_w, conv_b[None, :], prow, pcol, dskip_e,
                ssd_norm[None, :], e64, bsz=bsz, seq=seq, T=256)

    ct = c[:, SSD_HEADS:SSD_HEADS + FOX_HEADS].T.reshape(FOX_HEADS // 2, 2, -1)
    c8 = jnp.pad(ct, ((0, 0), (0, 6), (0, 0)))

    o = _fox(proj, vt, c8, c, bsz=bsz, seq=seq, tq=2048, tk=512,
             q_cb0=4 * CB, k_cb0=5 * CB, c_lane0=SSD_HEADS)
    return _outproj0(y, o, proj, x2, w_out.astype(BF16), g_post[None, :], tm=512)


def _odd_layer(x2, bsz, seq, g_pre, w_in, conv_w, conv_b, ln_g, ln_b, w_out, g_post):
    seg = 512
    cols = []
    for j in range(D_CONV // seg):
        for part in range(3):
            cols.append(w_in[:, part * D_CONV + j * seg:part * D_CONV + (j + 1) * seg])
    w = jnp.concatenate(cols, axis=1).astype(BF16)
    h, z = _inproj1(x2, g_pre[None, :], w, tm=1024, seg=seg)
    return _conv1(h, z, x2, conv_w, conv_b[None, :], ln_g[None, :], ln_b[None, :],
                  w_out.astype(BF16), g_post[None, :], bsz=bsz, seq=seq, T=256)


def kernel(x, e_norm_pre, e_w_in, e_conv_w, e_conv_b, e_dt_bias, e_a_log, e_d_skip, e_fgate_b,
           e_ssd_norm, e_w_out, e_norm_post, o_norm_pre, o_w_in, o_conv_w, o_conv_b, o_ln_g,
           o_ln_b, o_w_out, o_norm_post):
    bsz, seq, d = x.shape
    x2 = x.reshape(bsz * seq, d)
    depth = e_w_in.shape[0] + o_w_in.shape[0]
    for layer in range(depth):
        i = layer // 2
        if layer % 2 == 0:
            x2 = _even_layer(x2, bsz, seq, e_norm_pre[i], e_w_in[i], e_conv_w[i], e_conv_b[i],
                             e_dt_bias[i], e_a_log[i], e_d_skip[i], e_fgate_b[i],
                             e_ssd_norm[i], e_w_out[i], e_norm_post[i])
        else:
            x2 = _odd_layer(x2, bsz, seq, o_norm_pre[i], o_w_in[i], o_conv_w[i], o_conv_b[i],
                            o_ln_g[i], o_ln_b[i], o_w_out[i], o_norm_post[i])
    return x2.reshape(bsz, seq, d)
```

```python
import functools
import math

import jax
import jax.numpy as jnp
from jax import lax
from jax.experimental import pallas as pl
from jax.experimental.pallas import tpu as pltpu

F32 = jnp.float32
BF16 = jnp.bfloat16

LANES = 128
SUBLANES = 8
D_MODEL = 1024
SSD_HEADS = 16
SSD_GROUPS = 4
HEADDIM = 64
D_STATE = 128
CHUNK = 128
SSD_CONV = 4
FOX_HEADS = 16
D_CONV = 2 * D_MODEL
CONV_WIDTH = 31
EPS = 1e-6
NEG_BIG = -1e30
LOG2E = 1.4426950408889634
STRIP = 256
ONES_ROWS = 16
CONV_HALO = 32
LOOKAHEAD = 4
FOX_UNROLL = 2
VMEM_LIMIT = 56 * 1024 * 1024

CB = D_MODEL // LANES


def _dot(a, b):
    return jnp.dot(a, b, preferred_element_type=F32)


def _dot_nt(a, b):
    return lax.dot_general(a, b, (((1,), (1,)), ((), ())), preferred_element_type=F32)


def _dot_tn(a, b):
    return lax.dot_general(a, b, (((0,), (0,)), ((), ())), preferred_element_type=F32)


def _split(a, n):
    pieces = []
    r = a
    for _ in range(n):
        p = r.astype(BF16)
        pieces.append(p)
        r = r - p.astype(F32)
    return pieces


def _dot_split_lhs(a, m_bf16, n):
    out = None
    for p in _split(a, n):
        t = _dot(p, m_bf16)
        out = t if out is None else out + t
    return out


def _dot_split_rhs(m_bf16, a, n):
    out = None
    for p in _split(a, n):
        t = _dot(m_bf16, p)
        out = t if out is None else out + t
    return out


def _sigmoid(x):
    return 1.0 / (1.0 + jnp.exp(-x))


def _silu(x):
    return x * _sigmoid(x)


def _softplus(x):
    return jnp.maximum(x, 0.0) + jnp.log(1.0 + jnp.exp(-jnp.abs(x)))


def _log_sigmoid(x):
    return jnp.minimum(x, 0.0) - jnp.log(1.0 + jnp.exp(-jnp.abs(x)))


def _cat_blocks(ref, lo, hi, rows=None):
    if rows is None:
        return jnp.concatenate([ref[c] for c in range(lo, hi)], axis=-1)
    return jnp.concatenate([ref[c, rows, :] for c in range(lo, hi)], axis=-1)


def _inproj0_kernel(x_ref, g_ref, w_ref, wvt_ref, wsh_ref, wsl_ref,
                    proj_ref, vt_ref, sm_ref, smt_ref, xn_scr, *, ncb):
    j = pl.program_id(1)

    @pl.when(j == 0)
    def _():
        x = x_ref[...]
        ms = jnp.mean(x * x, axis=-1, keepdims=True)
        xn = x * lax.rsqrt(ms + EPS) * g_ref[...]
        hi = xn.astype(BF16)
        lo = (xn - hi.astype(F32)).astype(BF16)
        xn_scr[...] = hi
        small = _dot(hi, wsh_ref[...]) + _dot(lo, wsh_ref[...]) + _dot(hi, wsl_ref[...])
        sm_ref[...] = small
        smt_ref[...] = small.T
        vt = _dot_nt(wvt_ref[...], hi)
        for c in range(CB):
            vt_ref[c] = vt[c * LANES:(c + 1) * LANES, :].astype(BF16)

    acc = _dot(xn_scr[...], w_ref[...])
    for c in range(ncb):
        proj_ref[c] = acc[:, c * LANES:(c + 1) * LANES].astype(BF16)


def _inproj0(x2, g, w_main, wvt, wsh, wsl, *, tm, tn):
    m = x2.shape[0]
    ncol = w_main.shape[1] // tn
    ncb = tn // LANES
    kern = functools.partial(_inproj0_kernel, ncb=ncb)
    return pl.pallas_call(
        kern,
        grid=(m // tm, ncol),
        in_specs=[
            pl.BlockSpec((tm, D_MODEL), lambda i, j: (i, 0)),
            pl.BlockSpec((1, D_MODEL), lambda i, j: (0, 0)),
            pl.BlockSpec((D_MODEL, tn), lambda i, j: (0, j)),
            pl.BlockSpec((D_MODEL, D_MODEL), lambda i, j: (0, 0)),
            pl.BlockSpec((D_MODEL, LANES), lambda i, j: (0, 0)),
            pl.BlockSpec((D_MODEL, LANES), lambda i, j: (0, 0)),
        ],
        out_specs=[
            pl.BlockSpec((ncb, tm, LANES), lambda i, j: (j, i, 0)),
            pl.BlockSpec((CB, LANES, tm), lambda i, j: (0, 0, i)),
            pl.BlockSpec((tm, LANES), lambda i, j: (i, 0)),
            pl.BlockSpec((LANES, tm), lambda i, j: (0, i)),
        ],
        out_shape=[
            jax.ShapeDtypeStruct((w_main.shape[1] // LANES, m, LANES), BF16),
            jax.ShapeDtypeStruct((CB, LANES, m), BF16),
            jax.ShapeDtypeStruct((m, LANES), F32),
            jax.ShapeDtypeStruct((LANES, m), F32),
        ],
        scratch_shapes=[pltpu.VMEM((tm, D_MODEL), BF16)],
        compiler_params=pltpu.CompilerParams(
            dimension_semantics=("arbitrary", "arbitrary"), vmem_limit_bytes=VMEM_LIMIT),
        name="inproj0",
    )(x2, g, w_main, wvt, wsh, wsl)


def _ssd_kernel(xbc_ref, z_ref, sm_ref, smt_ref, convw_ref, convb_ref, prow_ref, pcol_ref,
                dskip_ref, norm_ref, e64_ref,
                y_ref, c_ref,
                h_scr, ext_scr, xc_scr, ccarry_scr, sh_scr, *, T):
    t = pl.program_id(1)
    nxc = 2 * CB

    @pl.when(t == 0)
    def _():
        h_scr[...] = jnp.zeros_like(h_scr)
        ext_scr[0:8, :] = jnp.zeros((8, nxc * LANES), F32)
        ccarry_scr[...] = jnp.zeros_like(ccarry_scr)

    for c in range(nxc):
        ext_scr[8:8 + T, c * LANES:(c + 1) * LANES] = xbc_ref[c].astype(F32)
    for c in range(nxc):
        cs_ = slice(c * LANES, (c + 1) * LANES)
        for k in range(SSD_CONV - 1):
            off = SUBLANES - (SSD_CONV - 1) + k
            sh_scr[k] = ext_scr[off:off + T, cs_]
        acc = convb_ref[:, cs_] + (ext_scr[SUBLANES:SUBLANES + T, cs_]
                                   * convw_ref[SSD_CONV - 1:SSD_CONV, cs_])
        for k in range(SSD_CONV - 1):
            acc = acc + sh_scr[k] * convw_ref[k:k + 1, cs_]
        xc_scr[:, cs_] = _silu(acc)
    ext_scr[0:8, :] = ext_scr[T:T + 8, :]

    dtb_row = prow_ref[0:1, :]
    a_row = -jnp.exp(prow_ref[1:2, :])
    fb_row = prow_ref[2:3, :]
    dtb_col = pcol_ref[0]
    a_col = -jnp.exp(pcol_ref[1])

    rt = lax.broadcasted_iota(jnp.int32, (T, T), 0)
    ct = lax.broadcasted_iota(jnp.int32, (T, T), 1)
    ltri_t = jnp.where(rt >= ct, 1.0, 0.0).astype(BF16)
    logf = _log_sigmoid(sm_ref[...] + fb_row)
    ctile = _dot_split_rhs(ltri_t, logf, 3) + ccarry_scr[...]
    c_ref[...] = ctile
    ccarry_scr[...] = ctile[T - 1:T, :]

    ri = lax.broadcasted_iota(jnp.int32, (CHUNK, CHUNK), 0)
    ci = lax.broadcasted_iota(jnp.int32, (CHUNK, CHUNK), 1)
    causal = ri >= ci
    ltri = jnp.where(causal, 1.0, 0.0).astype(BF16)
    utri = jnp.where(ri <= ci, 1.0, 0.0).astype(BF16)
    lane = lax.broadcasted_iota(jnp.int32, (CHUNK, LANES), 1)
    first_head = lane < HEADDIM
    e64 = e64_ref[...]

    for ck in range(T // CHUNK):
        r = slice(ck * CHUNK, (ck + 1) * CHUNK)
        dt = _softplus(sm_ref[r, :] + dtb_row)
        cs = _dot_split_rhs(ltri, dt * a_row, 3)
        dtt = _softplus(smt_ref[0:SSD_HEADS, r] + dtb_col)
        cst = _dot_split_lhs(dtt * a_col, utri, 3)
        cs_e = _dot_split_lhs(cs, e64, 3)
        dt_e = _dot_split_lhs(dt, e64, 3)
        ecs_e = jnp.exp(cs_e)
        dte_e = jnp.exp(cs_e[CHUNK - 1:CHUNK, :] - cs_e)

        for g in range(SSD_GROUPS):
            bsl = slice(D_MODEL + g * D_STATE, D_MODEL + (g + 1) * D_STATE)
            csl = slice(D_MODEL + (SSD_GROUPS + g) * D_STATE,
                        D_MODEL + (SSD_GROUPS + g + 1) * D_STATE)
            bg = xc_scr[r, bsl].astype(BF16)
            cg = xc_scr[r, csl].astype(BF16)
            cb_ = _dot_nt(cg, bg)
            ys = []
            for pp in range(2):
                p = 2 * g + pp
                ps = slice(p * LANES, (p + 1) * LANES)
                x = xc_scr[r, ps]
                xd = x * dt_e[:, ps]
                xd_b = xd.astype(BF16)
                yd = []
                for hh in range(2):
                    h = 2 * p + hh
                    colb = jnp.broadcast_to(cs[:, h:h + 1], (CHUNK, CHUNK))
                    rowb = jnp.broadcast_to(cst[h:h + 1, :], (CHUNK, CHUNK))
                    lm = jnp.where(causal, jnp.exp(colb - rowb), 0.0)
                    yd.append(_dot((cb_ * lm).astype(BF16), xd_b))
                ydiag = jnp.where(first_head, yd[0], yd[1])
                hin = h_scr[:, ps]
                yoff = _dot(cg, hin.astype(BF16)) * ecs_e[:, ps]
                snew = _dot_tn(bg, (xd * dte_e[:, ps]).astype(BF16))
                h_scr[:, ps] = hin * ecs_e[CHUNK - 1:CHUNK, ps] + snew
                yp = ydiag + yoff + x * dskip_ref[:, ps]
                yp = yp * _silu(z_ref[p, r, :].astype(F32))
                ys.append(yp)
            ss = (jnp.sum(ys[0] * ys[0], axis=-1, keepdims=True)
                  + jnp.sum(ys[1] * ys[1], axis=-1, keepdims=True))
            scale = lax.rsqrt(ss * (1.0 / (2 * LANES)) + EPS)
            for pp in range(2):
                p = 2 * g + pp
                ps = slice(p * LANES, (p + 1) * LANES)
                y_ref[p, r, :] = (ys[pp] * scale * norm_ref[:, ps]).astype(BF16)


def _ssd(proj, sm, smt, convw, convb, prow, pcol, dskip_e, norm, e64, *, bsz, seq, T):
    m = bsz * seq
    nt = seq // T
    kern = functools.partial(_ssd_kernel, T=T)
    const2 = lambda b, t: (0, 0)
    return pl.pallas_call(
        kern,
        grid=(bsz, nt),
        in_specs=[
            pl.BlockSpec((2 * CB, T, LANES), lambda b, t: (1, b * nt + t, 0)),
            pl.BlockSpec((CB, T, LANES), lambda b, t: (0, b * nt + t, 0)),
            pl.BlockSpec((T, LANES), lambda b, t: (b * nt + t, 0)),
            pl.BlockSpec((LANES, T), lambda b, t: (0, b * nt + t)),
            pl.BlockSpec(convw.shape, const2),
            pl.BlockSpec(convb.shape, const2),
            pl.BlockSpec(prow.shape, const2),
            pl.BlockSpec(pcol.shape, lambda b, t: (0, 0, 0)),
            pl.BlockSpec(dskip_e.shape, const2),
            pl.BlockSpec(norm.shape, const2),
            pl.BlockSpec(e64.shape, const2),
        ],
        out_specs=[
            pl.BlockSpec((CB, T, LANES), lambda b, t: (0, b * nt + t, 0)),
            pl.BlockSpec((T, LANES), lambda b, t: (b * nt + t, 0)),
        ],
        out_shape=[
            jax.ShapeDtypeStruct((CB, m, LANES), BF16),
            jax.ShapeDtypeStruct((m, LANES), F32),
        ],
        scratch_shapes=[
            pltpu.VMEM((D_STATE, D_MODEL), F32),
            pltpu.VMEM((T + 8, 2 * D_MODEL), F32),
            pltpu.VMEM((T, 2 * D_MODEL), F32),
            pltpu.VMEM((1, LANES), F32),
            pltpu.VMEM((SSD_CONV - 1, T, LANES), F32),
        ],
        compiler_params=pltpu.CompilerParams(
            dimension_semantics=("arbitrary", "arbitrary"), vmem_limit_bytes=VMEM_LIMIT),
        name="ssd",
    )(proj, proj, sm, smt, convw, convb, prow, pcol, dskip_e, norm, e64)


def _fox_kernel(q_ref, k_ref, vt_ref, cq_ref, c_ref, o_ref, qh_scr, m_scr, acc_scr, *,
                tq, tk, c_lane0):
    pair = pl.program_id(1)
    i = pl.program_id(2)
    nstrip = tq // STRIP
    lane_k = lax.broadcasted_iota(jnp.int32, (tk, LANES), 1)
    lane = lax.broadcasted_iota(jnp.int32, (tq, LANES), 1)
    q = q_ref[...]
    zero = jnp.zeros_like(q)
    qh_scr[0] = jnp.where(lane < HEADDIM, q, zero)
    qh_scr[1] = jnp.where(lane >= HEADDIM, q, zero)
    m_scr[...] = jnp.full_like(m_scr, NEG_BIG)
    acc_scr[...] = jnp.zeros_like(acc_scr)
    ones = jnp.ones((ONES_ROWS, tk), BF16)

    def scores(hh, st, nkv, k):
        return _dot_nt(k[0:nkv], qh_scr[hh, st * STRIP:(st + 1) * STRIP, :])

    def softmax_pv(hh, st, nkv, s, ck2, vta, koff, partial):
        cols = slice(st * STRIP, (st + 1) * STRIP)
        t = s - ck2[0:nkv]
        vta = vta[:, 0:nkv]
        if partial:
            kv_pos = koff + lax.broadcasted_iota(jnp.int32, (nkv, STRIP), 0)
            q_pos = i * tq + st * STRIP + lax.broadcasted_iota(jnp.int32, (nkv, STRIP), 1)
            t = jnp.where(kv_pos <= q_pos, t, NEG_BIG)
        cq2 = cq_ref[hh:hh + 1, cols] * LOG2E
        m_prev = m_scr[hh, :, cols]
        m_new = jnp.maximum(m_prev, jnp.max(t, axis=0, keepdims=True) + cq2)
        alpha = jnp.exp2(m_prev - m_new)
        p = jnp.exp2((t - (m_new - cq2)).astype(BF16))
        acc_scr[hh, :, cols] = acc_scr[hh, :, cols] * alpha + _dot(vta, p)
        m_scr[hh, :, cols] = m_new

    def strips_of(diag):
        items = []
        for hh in range(2):
            for st in range(nstrip):
                nkv, partial = tk, False
                if diag is not None:
                    kv_lo, q_lo = diag * tk, st * STRIP
                    if kv_lo > q_lo + STRIP - 1:
                        continue
                    nkv = min(tk, q_lo + STRIP - kv_lo)
                    partial = kv_lo + nkv - 1 > q_lo
                items.append((hh, st, nkv, partial))
        return items

    def blocks(j0, diags):
        koffs, ks, ck2s, vtas, items = [], [], [], [], []
        for b, diag in enumerate(diags):
            koff = pl.multiple_of((j0 + b) * tk, tk)
            cblk = c_ref[pl.ds(koff, tk), :]
            ck2, vta = [], []
            for hh in range(2):
                sel = lane_k == (c_lane0 + 2 * pair + hh)
                ck2.append(jnp.sum(jnp.where(sel, cblk, 0.0), axis=1, keepdims=True) * LOG2E)
                vta.append(jnp.concatenate(
                    [vt_ref[hh * HEADDIM:(hh + 1) * HEADDIM, pl.ds(koff, tk)], ones], axis=0))
            koffs.append(koff)
            ks.append(k_ref[pl.ds(koff, tk), :])
            ck2s.append(ck2)
            vtas.append(vta)
            items += [(b,) + it for it in strips_of(diag)]
        pending = [scores(hh, st, nkv, ks[b]) for b, hh, st, nkv, _ in items[:LOOKAHEAD]]
        for n, (b, hh, st, nkv, partial) in enumerate(items):
            if n + LOOKAHEAD < len(items):
                nb, nhh, nst, nnkv, _ = items[n + LOOKAHEAD]
                pending.append(scores(nhh, nst, nnkv, ks[nb]))
            softmax_pv(hh, st, nkv, pending.pop(0), ck2s[b][hh], vtas[b][hh], koffs[b], partial)

    ndiag = tq // tk
    nfull = i * ndiag
    unroll = math.gcd(ndiag, FOX_UNROLL)

    def body(j, carry):
        blocks(j * unroll, [None] * unroll)
        return carry

    lax.fori_loop(0, nfull // unroll, body, 0)
    blocks(nfull, list(range(ndiag)))

    outs = []
    for hh in range(2):
        acc = acc_scr[hh]
        outs.append(acc[0:HEADDIM, :] * (1.0 / acc[HEADDIM:HEADDIM + 1, :]))
    o_ref[...] = jnp.concatenate(outs, axis=0).T.astype(BF16)


def _fox(proj, vt, c8, c, *, bsz, seq, tq, tk, q_cb0, k_cb0, c_lane0):
    m = bsz * seq
    nq = seq // tq
    npair = FOX_HEADS // 2
    kern = functools.partial(_fox_kernel, tq=tq, tk=tk, c_lane0=c_lane0)
    return pl.pallas_call(
        kern,
        grid=(bsz, npair, nq),
        in_specs=[
            pl.BlockSpec((None, tq, LANES), lambda b, p, i: (q_cb0 + p, b * nq + i, 0)),
            pl.BlockSpec((None, seq, LANES), lambda b, p, i: (k_cb0 + p, b, 0)),
            pl.BlockSpec((None, LANES, seq), lambda b, p, i: (p, 0, b)),
            pl.BlockSpec((None, 8, tq), lambda b, p, i: (p, 0, b * nq + i)),
            pl.BlockSpec((seq, LANES), lambda b, p, i: (b, 0)),
        ],
        out_specs=pl.BlockSpec((None, tq, LANES), lambda b, p, i: (p, b * nq + i, 0)),
        out_shape=jax.ShapeDtypeStruct((npair, m, LANES), BF16),
        scratch_shapes=[
            pltpu.VMEM((2, tq, LANES), BF16),
            pltpu.VMEM((2, 1, tq), F32),
            pltpu.VMEM((2, HEADDIM + ONES_ROWS, tq), F32),
        ],
        compiler_params=pltpu.CompilerParams(
            dimension_semantics=("arbitrary", "arbitrary", "arbitrary"),
            vmem_limit_bytes=VMEM_LIMIT),
        name="fox",
    )(proj, proj, vt, c8, c)


def _outproj0_kernel(y_ref, o_ref, z_ref, x_ref, w_ref, g_ref, out_ref):
    y = _cat_blocks(y_ref, 0, CB)
    o = _cat_blocks(o_ref, 0, CB).astype(F32)
    z = _cat_blocks(z_ref, 0, CB).astype(F32)
    og = (o * _silu(z)).astype(BF16)
    acc = _dot(y, w_ref[0:D_MODEL, :]) + _dot(og, w_ref[D_MODEL:2 * D_MODEL, :])
    ms = jnp.mean(acc * acc, axis=-1, keepdims=True)
    out_ref[...] = x_ref[...] + acc * lax.rsqrt(ms + EPS) * g_ref[...]


def _outproj0(y, o, proj, x2, w, g, *, tm):
    m = x2.shape[0]
    return pl.pallas_call(
        _outproj0_kernel,
        grid=(m // tm,),
        in_specs=[
            pl.BlockSpec((CB, tm, LANES), lambda i: (0, i, 0)),
            pl.BlockSpec((CB, tm, LANES), lambda i: (0, i, 0)),
            pl.BlockSpec((CB, tm, LANES), lambda i: (1, i, 0)),
            pl.BlockSpec((tm, D_MODEL), lambda i: (i, 0)),
            pl.BlockSpec(w.shape, lambda i: (0, 0)),
            pl.BlockSpec((1, D_MODEL), lambda i: (0, 0)),
        ],
        out_specs=pl.BlockSpec((tm, D_MODEL), lambda i: (i, 0)),
        out_shape=jax.ShapeDtypeStruct((m, D_MODEL), F32),
        compiler_params=pltpu.CompilerParams(
            dimension_semantics=("arbitrary",), vmem_limit_bytes=VMEM_LIMIT),
        name="outproj0",
    )(y, o, proj, x2, w, g)


def _inproj1_kernel(x_ref, g_ref, w_ref, h_ref, z_ref, xn_scr, *, seg):
    j = pl.program_id(1)

    @pl.when(j == 0)
    def _():
        x = x_ref[...]
        ms = jnp.mean(x * x, axis=-1, keepdims=True)
        xn_scr[...] = (x * lax.rsqrt(ms + EPS) * g_ref[...]).astype(BF16)

    acc = _dot(xn_scr[...], w_ref[...])
    hv = acc[:, 0:seg] * _sigmoid(acc[:, seg:2 * seg])
    for c in range(seg // LANES):
        h_ref[c] = hv[:, c * LANES:(c + 1) * LANES].astype(BF16)
        z_ref[c] = acc[:, 2 * seg + c * LANES:2 * seg + (c + 1) * LANES].astype(BF16)


def _inproj1(x2, g, w, *, tm, seg):
    m = x2.shape[0]
    tn = 3 * seg
    ncol = w.shape[1] // tn
    scb = seg // LANES
    kern = functools.partial(_inproj1_kernel, seg=seg)
    return pl.pallas_call(
        kern,
        grid=(m // tm, ncol),
        in_specs=[
            pl.BlockSpec((tm, D_MODEL), lambda i, j: (i, 0)),
            pl.BlockSpec((1, D_MODEL), lambda i, j: (0, 0)),
            pl.BlockSpec((D_MODEL, tn), lambda i, j: (0, j)),
        ],
        out_specs=[
            pl.BlockSpec((scb, tm, LANES), lambda i, j: (j, i, 0)),
            pl.BlockSpec((scb, tm, LANES), lambda i, j: (j, i, 0)),
        ],
        out_shape=[
            jax.ShapeDtypeStruct((D_CONV // LANES, m, LANES), BF16),
            jax.ShapeDtypeStruct((D_CONV // LANES, m, LANES), BF16),
        ],
        scratch_shapes=[pltpu.VMEM((tm, D_MODEL), BF16)],
        compiler_params=pltpu.CompilerParams(
            dimension_semantics=("arbitrary", "arbitrary"), vmem_limit_bytes=VMEM_LIMIT),
        name="inproj1",
    )(x2, g, w)


def _conv1_kernel(h_ref, z_ref, x_ref, cw_ref, cb_ref, lg_ref, lb_ref, w_ref, g_ref,
                  out_ref, ext_scr, hc_scr, sh_scr, *, T):
    t = pl.program_id(1)
    ncb = D_CONV // LANES
    halo = CONV_HALO

    @pl.when(t == 0)
    def _():
        ext_scr[0:halo, :] = jnp.zeros((halo, D_CONV), F32)

    for c in range(ncb):
        ext_scr[halo:halo + T, c * LANES:(c + 1) * LANES] = h_ref[c].astype(F32)
    span = T + halo - SUBLANES
    for c in range(ncb):
        cs_ = slice(c * LANES, (c + 1) * LANES)
        for r in range(1, SUBLANES):
            sh_scr[r - 1] = ext_scr[r:r + span, cs_]
        acc = cb_ref[:, cs_]
        for k in range(CONV_WIDTH):
            base, r = divmod(halo - (CONV_WIDTH - 1) + k, SUBLANES)
            base *= SUBLANES
            if r == 0:
                src = ext_scr[base:base + T, cs_]
            else:
                src = sh_scr[r - 1, base:base + T, :]
            acc = acc + src * cw_ref[k:k + 1, cs_]
        hc_scr[:, cs_] = acc
    ext_scr[0:halo, :] = ext_scr[T:T + halo, :]

    hc = hc_scr[...]
    mu = jnp.mean(hc, axis=-1, keepdims=True)
    xc = hc - mu
    var = jnp.mean(xc * xc, axis=-1, keepdims=True)
    hn = xc * lax.rsqrt(var + EPS) * lg_ref[...] + lb_ref[...]
    z = _cat_blocks(z_ref, 0, ncb).astype(F32)
    hg = (_silu(hn) * _silu(z)).astype(BF16)
    acc = _dot(hg, w_ref[...])
    ms = jnp.mean(acc * acc, axis=-1, keepdims=True)
    out_ref[...] = x_ref[...] + acc * lax.rsqrt(ms + EPS) * g_ref[...]


def _conv1(h, z, x2, cw, cb, lg, lb, w, g, *, bsz, seq, T):
    m = bsz * seq
    nt = seq // T
    ncb = D_CONV // LANES
    kern = functools.partial(_conv1_kernel, T=T)
    const2 = lambda b, t: (0, 0)
    return pl.pallas_call(
        kern,
        grid=(bsz, nt),
        in_specs=[
            pl.BlockSpec((ncb, T, LANES), lambda b, t: (0, b * nt + t, 0)),
            pl.BlockSpec((ncb, T, LANES), lambda b, t: (0, b * nt + t, 0)),
            pl.BlockSpec((T, D_MODEL), lambda b, t: (b * nt + t, 0)),
            pl.BlockSpec(cw.shape, const2),
            pl.BlockSpec(cb.shape, const2),
            pl.BlockSpec(lg.shape, const2),
            pl.BlockSpec(lb.shape, const2),
            pl.BlockSpec(w.shape, const2),
            pl.BlockSpec(g.shape, const2),
        ],
        out_specs=pl.BlockSpec((T, D_MODEL), lambda b, t: (b * nt + t, 0)),
        out_shape=jax.ShapeDtypeStruct((m, D_MODEL), F32),
        scratch_shapes=[
            pltpu.VMEM((T + CONV_HALO, D_CONV), F32),
            pltpu.VMEM((T, D_CONV), F32),
            pltpu.VMEM((SUBLANES - 1, T + CONV_HALO - SUBLANES, LANES), F32),
        ],
        compiler_params=pltpu.CompilerParams(
            dimension_semantics=("arbitrary", "arbitrary"), vmem_limit_bytes=VMEM_LIMIT),
        name="conv1",
    )(h, z, x2, cw, cb, lg, lb, w, g)


def _pad_lanes(v, width=LANES):
    return jnp.pad(v, (0, width - v.shape[0]))


def _even_layer(x2, bsz, seq, g_pre, w_in, conv_w, conv_b, dt_bias, a_log, d_skip, fgate_b,
                ssd_norm, w_out, g_post):
    d_xbc = D_MODEL + 2 * SSD_GROUPS * D_STATE
    o_z, o_xbc = 0, 2 * D_MODEL
    o_dt = o_xbc + d_xbc
    o_q = o_dt + SSD_HEADS
    o_k = o_q + D_MODEL
    o_v = o_k + D_MODEL
    o_f = o_v + D_MODEL
    q_scale = HEADDIM ** -0.5 * LOG2E
    w_main = jnp.concatenate(
        [w_in[:, o_z:o_dt], w_in[:, o_q:o_k] * q_scale, w_in[:, o_k:o_v]],
        axis=1).astype(BF16)
    wvt = w_in[:, o_v:o_f].T.astype(BF16)
    w_small = jnp.concatenate(
        [w_in[:, o_dt:o_q], w_in[:, o_f:o_f + FOX_HEADS],
         jnp.zeros((D_MODEL, LANES - SSD_HEADS - FOX_HEADS), F32)], axis=1)
    wsh = w_small.astype(BF16)
    wsl = (w_small - wsh.astype(F32)).astype(BF16)

    proj, vt, sm, smt = _inproj0(x2, g_pre[None, :], w_main, wvt, wsh, wsl, tm=1024, tn=1024)

    zeros16 = jnp.zeros((SSD_HEADS,), F32)
    prow = jnp.stack([
        _pad_lanes(dt_bias),
        _pad_lanes(a_log),
        _pad_lanes(jnp.concatenate([zeros16, fgate_b])),
    ] + [jnp.zeros((LANES,), F32)] * 5)
    pcol = jnp.stack([jnp.broadcast_to(dt_bias[:, None], (SSD_HEADS, LANES)),
                      jnp.broadcast_to(a_log[:, None], (SSD_HEADS, LANES))])
    dskip_e = jnp.repeat(d_skip, HEADDIM)[None, :]
    head_of_lane = jnp.arange(D_MODEL) // HEADDIM
    e64 = (jnp.arange(LANES)[:, None] == head_of_lane[None, :]).astype(BF16)

    y, c = _ssd(proj, sm, smt, conv_w, conv_b[None, :], prow, pcol, dskip_e,
                ssd_norm[None, :], e64, bsz=bsz, seq=seq, T=256)

    ct = c[:, SSD_HEADS:SSD_HEADS + FOX_HEADS].T.reshape(FOX_HEADS // 2, 2, -1)
    c8 = jnp.pad(ct, ((0, 0), (0, 6), (0, 0)))

    o = _fox(proj, vt, c8, c, bsz=bsz, seq=seq, tq=2048, tk=512,
             q_cb0=4 * CB, k_cb0=5 * CB, c_lane0=SSD_HEADS)
    return _outproj0(y, o, proj, x2, w_out.astype(BF16), g_post[None, :], tm=512)


def _odd_layer(x2, bsz, seq, g_pre, w_in, conv_w, conv_b, ln_g, ln_b, w_out, g_post):
    seg = 512
    cols = []
    for j in range(D_CONV // seg):
        for part in range(3):
            cols.append(w_in[:, part * D_CONV + j * seg:part * D_CONV + (j + 1) * seg])
    w = jnp.concatenate(cols, axis=1).astype(BF16)
    h, z = _inproj1(x2, g_pre[None, :], w, tm=1024, seg=seg)
    return _conv1(h, z, x2, conv_w, conv_b[None, :], ln_g[None, :], ln_b[None, :],
                  w_out.astype(BF16), g_post[None, :], bsz=bsz, seq=seq, T=256)


def kernel(x, e_norm_pre, e_w_in, e_conv_w, e_conv_b, e_dt_bias, e_a_log, e_d_skip, e_fgate_b,
           e_ssd_norm, e_w_out, e_norm_post, o_norm_pre, o_w_in, o_conv_w, o_conv_b, o_ln_g,
           o_ln_b, o_w_out, o_norm_post):
    bsz, seq, d = x.shape
    x2 = x.reshape(bsz * seq, d)
    depth = e_w_in.shape[0] + o_w_in.shape[0]
    for layer in range(depth):
        i = layer // 2
        if layer % 2 == 0:
            x2 = _even_layer(x2, bsz, seq, e_norm_pre[i], e_w_in[i], e_conv_w[i], e_conv_b[i],
                             e_dt_bias[i], e_a_log[i], e_d_skip[i], e_fgate_b[i],
                             e_ssd_norm[i], e_w_out[i], e_norm_post[i])
        else:
            x2 = _odd_layer(x2, bsz, seq, o_norm_pre[i], o_w_in[i], o_conv_w[i], o_conv_b[i],
                            o_ln_g[i], o_ln_b[i], o_w_out[i], o_norm_post[i])
    return x2.reshape(bsz, seq, d)
```

```python
import functools
import math
from typing import NamedTuple

import jax
import jax.numpy as jnp
from jax import lax
from jax.experimental import pallas as pl
from jax.experimental.pallas import tpu as pltpu

F32 = jnp.float32
BF16 = jnp.bfloat16

LANES = 128
SUBLANES = 8
D_MODEL = 1024
SSD_HEADS = 16
SSD_GROUPS = 4
HEADDIM = 64
D_STATE = 128
CHUNK = 128
SSD_CONV = 4
FOX_HEADS = 16
D_CONV = 2 * D_MODEL
CONV_WIDTH = 31
EPS = 1e-6
NEG_BIG = -1e30
LOG2E = 1.4426950408889634
STRIP = 256
ONES_ROWS = 16
CONV_HALO = 32
LOOKAHEAD = 4
FOX_UNROLL = 4
VMEM_LIMIT = 56 * 1024 * 1024

CB = D_MODEL // LANES


class _Tiles(NamedTuple):
    proj_rows: int = 1024
    proj_cols: int = 1024
    glu_seg: int = 512
    ssd_rows: int = 256
    fox_q: int = 2048
    fox_kv: int = 512
    out_rows: int = 512
    conv_rows: int = 256


TILES = _Tiles()


def _dot(a, b):
    return jnp.dot(a, b, preferred_element_type=F32)


def _dot_nt(a, b):
    return lax.dot_general(a, b, (((1,), (1,)), ((), ())), preferred_element_type=F32)


def _dot_tn(a, b):
    return lax.dot_general(a, b, (((0,), (0,)), ((), ())), preferred_element_type=F32)


def _split(a, n):
    pieces = []
    r = a
    for _ in range(n):
        p = r.astype(BF16)
        pieces.append(p)
        r = r - p.astype(F32)
    return pieces


def _dot_split_lhs(a, m_bf16, n):
    out = None
    for p in _split(a, n):
        t = _dot(p, m_bf16)
        out = t if out is None else out + t
    return out


def _dot_split_rhs(m_bf16, a, n):
    out = None
    for p in _split(a, n):
        t = _dot(m_bf16, p)
        out = t if out is None else out + t
    return out


def _sigmoid(x):
    return 1.0 / (1.0 + jnp.exp(-x))


def _silu(x):
    return x * _sigmoid(x)


def _softplus(x):
    return jnp.maximum(x, 0.0) + jnp.log(1.0 + jnp.exp(-jnp.abs(x)))


def _log_sigmoid(x):
    return jnp.minimum(x, 0.0) - jnp.log(1.0 + jnp.exp(-jnp.abs(x)))


def _cat_blocks(ref, lo, hi):
    return jnp.concatenate([ref[c] for c in range(lo, hi)], axis=-1)


def _inproj0_kernel(x_ref, g_ref, w_ref, wvt_ref, wsh_ref, wsl_ref,
                    proj_ref, vt_ref, sm_ref, smt_ref, xn_scr, *, ncb):
    j = pl.program_id(1)

    @pl.when(j == 0)
    def _():
        x = x_ref[...]
        ms = jnp.mean(x * x, axis=-1, keepdims=True)
        xn = x * lax.rsqrt(ms + EPS) * g_ref[...]
        hi = xn.astype(BF16)
        lo = (xn - hi.astype(F32)).astype(BF16)
        xn_scr[...] = hi
        small = _dot(hi, wsh_ref[...]) + _dot(lo, wsh_ref[...]) + _dot(hi, wsl_ref[...])
        sm_ref[...] = small
        smt_ref[...] = small.T
        vt = _dot_nt(wvt_ref[...], hi)
        for c in range(CB):
            vt_ref[c] = vt[c * LANES:(c + 1) * LANES, :].astype(BF16)

    acc = _dot(xn_scr[...], w_ref[...])
    for c in range(ncb):
        proj_ref[c] = acc[:, c * LANES:(c + 1) * LANES].astype(BF16)


def _inproj0(x2, g, w_main, wvt, wsh, wsl, *, tm, tn):
    m = x2.shape[0]
    ncol = w_main.shape[1] // tn
    ncb = tn // LANES
    kern = functools.partial(_inproj0_kernel, ncb=ncb)
    return pl.pallas_call(
        kern,
        grid=(m // tm, ncol),
        in_specs=[
            pl.BlockSpec((tm, D_MODEL), lambda i, j: (i, 0)),
            pl.BlockSpec((1, D_MODEL), lambda i, j: (0, 0)),
            pl.BlockSpec((D_MODEL, tn), lambda i, j: (0, j)),
            pl.BlockSpec((D_MODEL, D_MODEL), lambda i, j: (0, 0)),
            pl.BlockSpec((D_MODEL, LANES), lambda i, j: (0, 0)),
            pl.BlockSpec((D_MODEL, LANES), lambda i, j: (0, 0)),
        ],
        out_specs=[
            pl.BlockSpec((ncb, tm, LANES), lambda i, j: (j, i, 0)),
            pl.BlockSpec((CB, LANES, tm), lambda i, j: (0, 0, i)),
            pl.BlockSpec((tm, LANES), lambda i, j: (i, 0)),
            pl.BlockSpec((LANES, tm), lambda i, j: (0, i)),
        ],
        out_shape=[
            jax.ShapeDtypeStruct((w_main.shape[1] // LANES, m, LANES), BF16),
            jax.ShapeDtypeStruct((CB, LANES, m), BF16),
            jax.ShapeDtypeStruct((m, LANES), F32),
            jax.ShapeDtypeStruct((LANES, m), F32),
        ],
        scratch_shapes=[pltpu.VMEM((tm, D_MODEL), BF16)],
        compiler_params=pltpu.CompilerParams(
            dimension_semantics=("arbitrary", "arbitrary"), vmem_limit_bytes=VMEM_LIMIT),
        name="inproj0",
    )(x2, g, w_main, wvt, wsh, wsl)


def _ssd_kernel(xbc_ref, z_ref, sm_ref, smt_ref, convw_ref, convb_ref, prow_ref, pcol_ref,
                dskip_ref, norm_ref, e64_ref,
                y_ref, c_ref,
                h_scr, ext_scr, xc_scr, ccarry_scr, sh_scr, *, T):
    t = pl.program_id(1)
    nxc = 2 * CB

    @pl.when(t == 0)
    def _():
        h_scr[...] = jnp.zeros_like(h_scr)
        ext_scr[0:8, :] = jnp.zeros((8, nxc * LANES), F32)
        ccarry_scr[...] = jnp.zeros_like(ccarry_scr)

    for c in range(nxc):
        ext_scr[8:8 + T, c * LANES:(c + 1) * LANES] = xbc_ref[c].astype(F32)
    for c in range(nxc):
        cs_ = slice(c * LANES, (c + 1) * LANES)
        for k in range(SSD_CONV - 1):
            off = SUBLANES - (SSD_CONV - 1) + k
            sh_scr[k] = ext_scr[off:off + T, cs_]
        acc = convb_ref[:, cs_] + (ext_scr[SUBLANES:SUBLANES + T, cs_]
                                   * convw_ref[SSD_CONV - 1:SSD_CONV, cs_])
        for k in range(SSD_CONV - 1):
            acc = acc + sh_scr[k] * convw_ref[k:k + 1, cs_]
        xc_scr[:, cs_] = _silu(acc)
    ext_scr[0:8, :] = ext_scr[T:T + 8, :]

    dtb_row = prow_ref[0:1, :]
    a_row = -jnp.exp(prow_ref[1:2, :])
    fb_row = prow_ref[2:3, :]
    dtb_col = pcol_ref[0]
    a_col = -jnp.exp(pcol_ref[1])

    rt = lax.broadcasted_iota(jnp.int32, (T, T), 0)
    ct = lax.broadcasted_iota(jnp.int32, (T, T), 1)
    ltri_t = jnp.where(rt >= ct, 1.0, 0.0).astype(BF16)
    logf = _log_sigmoid(sm_ref[...] + fb_row)
    ctile = _dot_split_rhs(ltri_t, logf, 3) + ccarry_scr[...]
    c_ref[...] = ctile
    ccarry_scr[...] = ctile[T - 1:T, :]

    ri = lax.broadcasted_iota(jnp.int32, (CHUNK, CHUNK), 0)
    ci = lax.broadcasted_iota(jnp.int32, (CHUNK, CHUNK), 1)
    causal = ri >= ci
    ltri = jnp.where(causal, 1.0, 0.0).astype(BF16)
    utri = jnp.where(ri <= ci, 1.0, 0.0).astype(BF16)
    lane = lax.broadcasted_iota(jnp.int32, (CHUNK, LANES), 1)
    first_head = lane < HEADDIM
    e64 = e64_ref[...]

    for ck in range(T // CHUNK):
        r = slice(ck * CHUNK, (ck + 1) * CHUNK)
        dt = _softplus(sm_ref[r, :] + dtb_row)
        cs = _dot_split_rhs(ltri, dt * a_row, 3)
        dtt = _softplus(smt_ref[0:SSD_HEADS, r] + dtb_col)
        cst = _dot_split_lhs(dtt * a_col, utri, 3)
        cs_e = _dot_split_lhs(cs, e64, 3)
        dt_e = _dot_split_lhs(dt, e64, 3)
        ecs_e = jnp.exp(cs_e)
        dte_e = jnp.exp(cs_e[CHUNK - 1:CHUNK, :] - cs_e)

        for g in range(SSD_GROUPS):
            bsl = slice(D_MODEL + g * D_STATE, D_MODEL + (g + 1) * D_STATE)
            csl = slice(D_MODEL + (SSD_GROUPS + g) * D_STATE,
                        D_MODEL + (SSD_GROUPS + g + 1) * D_STATE)
            bg = xc_scr[r, bsl].astype(BF16)
            cg = xc_scr[r, csl].astype(BF16)
            cb_ = _dot_nt(cg, bg)
            ys = []
            for pp in range(2):
                p = 2 * g + pp
                ps = slice(p * LANES, (p + 1) * LANES)
                x = xc_scr[r, ps]
                xd = x * dt_e[:, ps]
                xd_b = xd.astype(BF16)
                yd = []
                for hh in range(2):
                    h = 2 * p + hh
                    colb = jnp.broadcast_to(cs[:, h:h + 1], (CHUNK, CHUNK))
                    rowb = jnp.broadcast_to(cst[h:h + 1, :], (CHUNK, CHUNK))
                    lm = jnp.where(causal, jnp.exp(colb - rowb), 0.0)
                    yd.append(_dot((cb_ * lm).astype(BF16), xd_b))
                ydiag = jnp.where(first_head, yd[0], yd[1])
                hin = h_scr[:, ps]
                yoff = _dot(cg, hin.astype(BF16)) * ecs_e[:, ps]
                snew = _dot_tn(bg, (xd * dte_e[:, ps]).astype(BF16))
                h_scr[:, ps] = hin * ecs_e[CHUNK - 1:CHUNK, ps] + snew
                yp = ydiag + yoff + x * dskip_ref[:, ps]
                yp = yp * _silu(z_ref[p, r, :].astype(F32))
                ys.append(yp)
            ss = (jnp.sum(ys[0] * ys[0], axis=-1, keepdims=True)
                  + jnp.sum(ys[1] * ys[1], axis=-1, keepdims=True))
            scale = lax.rsqrt(ss * (1.0 / (2 * LANES)) + EPS)
            for pp in range(2):
                p = 2 * g + pp
                ps = slice(p * LANES, (p + 1) * LANES)
                y_ref[p, r, :] = (ys[pp] * scale * norm_ref[:, ps]).astype(BF16)


def _ssd(proj, sm, smt, convw, convb, prow, pcol, dskip_e, norm, e64, *, bsz, seq, T):
    m = bsz * seq
    nt = seq // T
    kern = functools.partial(_ssd_kernel, T=T)
    const2 = lambda b, t: (0, 0)
    return pl.pallas_call(
        kern,
        grid=(bsz, nt),
        in_specs=[
            pl.BlockSpec((2 * CB, T, LANES), lambda b, t: (1, b * nt + t, 0)),
            pl.BlockSpec((CB, T, LANES), lambda b, t: (0, b * nt + t, 0)),
            pl.BlockSpec((T, LANES), lambda b, t: (b * nt + t, 0)),
            pl.BlockSpec((LANES, T), lambda b, t: (0, b * nt + t)),
            pl.BlockSpec(convw.shape, const2),
            pl.BlockSpec(convb.shape, const2),
            pl.BlockSpec(prow.shape, const2),
            pl.BlockSpec(pcol.shape, lambda b, t: (0, 0, 0)),
            pl.BlockSpec(dskip_e.shape, const2),
            pl.BlockSpec(norm.shape, const2),
            pl.BlockSpec(e64.shape, const2),
        ],
        out_specs=[
            pl.BlockSpec((CB, T, LANES), lambda b, t: (0, b * nt + t, 0)),
            pl.BlockSpec((T, LANES), lambda b, t: (b * nt + t, 0)),
        ],
        out_shape=[
            jax.ShapeDtypeStruct((CB, m, LANES), BF16),
            jax.ShapeDtypeStruct((m, LANES), F32),
        ],
        scratch_shapes=[
            pltpu.VMEM((D_STATE, D_MODEL), F32),
            pltpu.VMEM((T + 8, 2 * D_MODEL), F32),
            pltpu.VMEM((T, 2 * D_MODEL), F32),
            pltpu.VMEM((1, LANES), F32),
            pltpu.VMEM((SSD_CONV - 1, T, LANES), F32),
        ],
        compiler_params=pltpu.CompilerParams(
            dimension_semantics=("arbitrary", "arbitrary"), vmem_limit_bytes=VMEM_LIMIT),
        name="ssd",
    )(proj, proj, sm, smt, convw, convb, prow, pcol, dskip_e, norm, e64)


def _fox_kernel(q_ref, k_ref, vt_ref, cq_ref, c_ref, o_ref, qh_scr, m_scr, acc_scr, *,
                tq, tk, c_lane0):
    pair = pl.program_id(1)
    i = pl.program_id(2)
    nstrip = tq // STRIP
    lane_k = lax.broadcasted_iota(jnp.int32, (tk, LANES), 1)
    lane = lax.broadcasted_iota(jnp.int32, (tq, LANES), 1)
    q = q_ref[...]
    zero = jnp.zeros_like(q)
    qh_scr[0] = jnp.where(lane < HEADDIM, q, zero)
    qh_scr[1] = jnp.where(lane >= HEADDIM, q, zero)
    m_scr[...] = jnp.full_like(m_scr, NEG_BIG)
    acc_scr[...] = jnp.zeros_like(acc_scr)
    ones = jnp.ones((ONES_ROWS, tk), BF16)

    def scores(hh, st, nkv, k):
        return _dot_nt(k[0:nkv], qh_scr[hh, st * STRIP:(st + 1) * STRIP, :])

    def softmax_pv(hh, st, nkv, s, ck2, vta, koff, partial):
        cols = slice(st * STRIP, (st + 1) * STRIP)
        t = s - ck2[0:nkv]
        vta = vta[:, 0:nkv]
        if partial:
            kv_pos = koff + lax.broadcasted_iota(jnp.int32, (nkv, STRIP), 0)
            q_pos = i * tq + st * STRIP + lax.broadcasted_iota(jnp.int32, (nkv, STRIP), 1)
            t = jnp.where(kv_pos <= q_pos, t, NEG_BIG)
        cq2 = cq_ref[hh:hh + 1, cols] * LOG2E
        m_prev = m_scr[hh, :, cols]
        m_new = jnp.maximum(m_prev, jnp.max(t, axis=0, keepdims=True) + cq2)
        alpha = jnp.exp2(m_prev - m_new)
        p = jnp.exp2((t - (m_new - cq2)).astype(BF16))
        acc_scr[hh, :, cols] = acc_scr[hh, :, cols] * alpha + _dot(vta, p)
        m_scr[hh, :, cols] = m_new

    def strips_of(diag):
        items = []
        for hh in range(2):
            for st in range(nstrip):
                nkv, partial = tk, False
                if diag is not None:
                    kv_lo, q_lo = diag * tk, st * STRIP
                    if kv_lo > q_lo + STRIP - 1:
                        continue
                    nkv = min(tk, q_lo + STRIP - kv_lo)
                    partial = kv_lo + nkv - 1 > q_lo
                items.append((hh, st, nkv, partial))
        return items

    def blocks(j0, diags):
        koffs, ks, ck2s, vtas, items = [], [], [], [], []
        for b, diag in enumerate(diags):
            koff = pl.multiple_of((j0 + b) * tk, tk)
            cblk = c_ref[pl.ds(koff, tk), :]
            ck2, vta = [], []
            for hh in range(2):
                sel = lane_k == (c_lane0 + 2 * pair + hh)
                ck2.append(jnp.sum(jnp.where(sel, cblk, 0.0), axis=1, keepdims=True) * LOG2E)
                vta.append(jnp.concatenate(
                    [vt_ref[hh * HEADDIM:(hh + 1) * HEADDIM, pl.ds(koff, tk)], ones], axis=0))
            koffs.append(koff)
            ks.append(k_ref[pl.ds(koff, tk), :])
            ck2s.append(ck2)
            vtas.append(vta)
            items += [(b,) + it for it in strips_of(diag)]
        pending = [scores(hh, st, nkv, ks[b]) for b, hh, st, nkv, _ in items[:LOOKAHEAD]]
        for n, (b, hh, st, nkv, partial) in enumerate(items):
            if n + LOOKAHEAD < len(items):
                nb, nhh, nst, nnkv, _ = items[n + LOOKAHEAD]
                pending.append(scores(nhh, nst, nnkv, ks[nb]))
            softmax_pv(hh, st, nkv, pending.pop(0), ck2s[b][hh], vtas[b][hh], koffs[b], partial)

    ndiag = tq // tk
    nfull = i * ndiag
    unroll = math.gcd(ndiag, FOX_UNROLL)

    def body(j, carry):
        blocks(j * unroll, [None] * unroll)
        return carry

    lax.fori_loop(0, nfull // unroll, body, 0)
    blocks(nfull, list(range(ndiag)))

    outs = []
    for hh in range(2):
        acc = acc_scr[hh]
        outs.append(acc[0:HEADDIM, :] * (1.0 / acc[HEADDIM:HEADDIM + 1, :]))
    o_ref[...] = jnp.concatenate(outs, axis=0).T.astype(BF16)


def _fox(proj, vt, c8, c, *, bsz, seq, tq, tk, q_cb0, k_cb0, c_lane0):
    m = bsz * seq
    nq = seq // tq
    npair = FOX_HEADS // 2
    kern = functools.partial(_fox_kernel, tq=tq, tk=tk, c_lane0=c_lane0)
    return pl.pallas_call(
        kern,
        grid=(bsz, npair, nq),
        in_specs=[
            pl.BlockSpec((None, tq, LANES), lambda b, p, i: (q_cb0 + p, b * nq + i, 0)),
            pl.BlockSpec((None, seq, LANES), lambda b, p, i: (k_cb0 + p, b, 0)),
            pl.BlockSpec((None, LANES, seq), lambda b, p, i: (p, 0, b)),
            pl.BlockSpec((None, 8, tq), lambda b, p, i: (p, 0, b * nq + i)),
            pl.BlockSpec((seq, LANES), lambda b, p, i: (b, 0)),
        ],
        out_specs=pl.BlockSpec((None, tq, LANES), lambda b, p, i: (p, b * nq + i, 0)),
        out_shape=jax.ShapeDtypeStruct((npair, m, LANES), BF16),
        scratch_shapes=[
            pltpu.VMEM((2, tq, LANES), BF16),
            pltpu.VMEM((2, 1, tq), F32),
            pltpu.VMEM((2, HEADDIM + ONES_ROWS, tq), F32),
        ],
        compiler_params=pltpu.CompilerParams(
            dimension_semantics=("arbitrary", "arbitrary", "arbitrary"),
            vmem_limit_bytes=VMEM_LIMIT),
        name="fox",
    )(proj, proj, vt, c8, c)


def _outproj0_kernel(y_ref, o_ref, z_ref, x_ref, w_ref, g_ref, out_ref):
    y = _cat_blocks(y_ref, 0, CB)
    o = _cat_blocks(o_ref, 0, CB).astype(F32)
    z = _cat_blocks(z_ref, 0, CB).astype(F32)
    og = (o * _silu(z)).astype(BF16)
    acc = _dot(y, w_ref[0:D_MODEL, :]) + _dot(og, w_ref[D_MODEL:2 * D_MODEL, :])
    ms = jnp.mean(acc * acc, axis=-1, keepdims=True)
    out_ref[...] = x_ref[...] + acc * lax.rsqrt(ms + EPS) * g_ref[...]


def _outproj0(y, o, proj, x2, w, g, *, tm):
    m = x2.shape[0]
    return pl.pallas_call(
        _outproj0_kernel,
        grid=(m // tm,),
        in_specs=[
            pl.BlockSpec((CB, tm, LANES), lambda i: (0, i, 0)),
            pl.BlockSpec((CB, tm, LANES), lambda i: (0, i, 0)),
            pl.BlockSpec((CB, tm, LANES), lambda i: (1, i, 0)),
            pl.BlockSpec((tm, D_MODEL), lambda i: (i, 0)),
            pl.BlockSpec(w.shape, lambda i: (0, 0)),
            pl.BlockSpec((1, D_MODEL), lambda i: (0, 0)),
        ],
        out_specs=pl.BlockSpec((tm, D_MODEL), lambda i: (i, 0)),
        out_shape=jax.ShapeDtypeStruct((m, D_MODEL), F32),
        compiler_params=pltpu.CompilerParams(
            dimension_semantics=("arbitrary",), vmem_limit_bytes=VMEM_LIMIT),
        name="outproj0",
    )(y, o, proj, x2, w, g)


def _inproj1_kernel(x_ref, g_ref, w_ref, h_ref, z_ref, xn_scr, *, seg):
    j = pl.program_id(1)

    @pl.when(j == 0)
    def _():
        x = x_ref[...]
        ms = jnp.mean(x * x, axis=-1, keepdims=True)
        xn_scr[...] = (x * lax.rsqrt(ms + EPS) * g_ref[...]).astype(BF16)

    acc = _dot(xn_scr[...], w_ref[...])
    hv = acc[:, 0:seg] * _sigmoid(acc[:, seg:2 * seg])
    for c in range(seg // LANES):
        h_ref[c] = hv[:, c * LANES:(c + 1) * LANES].astype(BF16)
        z_ref[c] = acc[:, 2 * seg + c * LANES:2 * seg + (c + 1) * LANES].astype(BF16)


def _inproj1(x2, g, w, *, tm, seg):
    m = x2.shape[0]
    tn = 3 * seg
    ncol = w.shape[1] // tn
    scb = seg // LANES
    kern = functools.partial(_inproj1_kernel, seg=seg)
    return pl.pallas_call(
        kern,
        grid=(m // tm, ncol),
        in_specs=[
            pl.BlockSpec((tm, D_MODEL), lambda i, j: (i, 0)),
            pl.BlockSpec((1, D_MODEL), lambda i, j: (0, 0)),
            pl.BlockSpec((D_MODEL, tn), lambda i, j: (0, j)),
        ],
        out_specs=[
            pl.BlockSpec((scb, tm, LANES), lambda i, j: (j, i, 0)),
            pl.BlockSpec((scb, tm, LANES), lambda i, j: (j, i, 0)),
        ],
        out_shape=[
            jax.ShapeDtypeStruct((D_CONV // LANES, m, LANES), BF16),
            jax.ShapeDtypeStruct((D_CONV // LANES, m, LANES), BF16),
        ],
        scratch_shapes=[pltpu.VMEM((tm, D_MODEL), BF16)],
        compiler_params=pltpu.CompilerParams(
            dimension_semantics=("arbitrary", "arbitrary"), vmem_limit_bytes=VMEM_LIMIT),
        name="inproj1",
    )(x2, g, w)


def _conv1_kernel(h_ref, z_ref, x_ref, cw_ref, cb_ref, lg_ref, lb_ref, w_ref, g_ref,
                  out_ref, ext_scr, hc_scr, sh_scr, *, T):
    t = pl.program_id(1)
    ncb = D_CONV // LANES
    halo = CONV_HALO

    @pl.when(t == 0)
    def _():
        ext_scr[0:halo, :] = jnp.zeros((halo, D_CONV), F32)

    for c in range(ncb):
        ext_scr[halo:halo + T, c * LANES:(c + 1) * LANES] = h_ref[c].astype(F32)
    span = T + halo - SUBLANES
    for c in range(ncb):
        cs_ = slice(c * LANES, (c + 1) * LANES)
        for r in range(1, SUBLANES):
            sh_scr[r - 1] = ext_scr[r:r + span, cs_]
        acc = cb_ref[:, cs_]
        for k in range(CONV_WIDTH):
            base, r = divmod(halo - (CONV_WIDTH - 1) + k, SUBLANES)
            base *= SUBLANES
            if r == 0:
                src = ext_scr[base:base + T, cs_]
            else:
                src = sh_scr[r - 1, base:base + T, :]
            acc = acc + src * cw_ref[k:k + 1, cs_]
        hc_scr[:, cs_] = acc
    ext_scr[0:halo, :] = ext_scr[T:T + halo, :]

    hc = hc_scr[...]
    mu = jnp.mean(hc, axis=-1, keepdims=True)
    xc = hc - mu
    var = jnp.mean(xc * xc, axis=-1, keepdims=True)
    hn = xc * lax.rsqrt(var + EPS) * lg_ref[...] + lb_ref[...]
    z = _cat_blocks(z_ref, 0, ncb).astype(F32)
    hg = (_silu(hn) * _silu(z)).astype(BF16)
    acc = _dot(hg, w_ref[...])
    ms = jnp.mean(acc * acc, axis=-1, keepdims=True)
    out_ref[...] = x_ref[...] + acc * lax.rsqrt(ms + EPS) * g_ref[...]


def _conv1(h, z, x2, cw, cb, lg, lb, w, g, *, bsz, seq, T):
    m = bsz * seq
    nt = seq // T
    ncb = D_CONV // LANES
    kern = functools.partial(_conv1_kernel, T=T)
    const2 = lambda b, t: (0, 0)
    return pl.pallas_call(
        kern,
        grid=(bsz, nt),
        in_specs=[
            pl.BlockSpec((ncb, T, LANES), lambda b, t: (0, b * nt + t, 0)),
            pl.BlockSpec((ncb, T, LANES), lambda b, t: (0, b * nt + t, 0)),
            pl.BlockSpec((T, D_MODEL), lambda b, t: (b * nt + t, 0)),
            pl.BlockSpec(cw.shape, const2),
            pl.BlockSpec(cb.shape, const2),
            pl.BlockSpec(lg.shape, const2),
            pl.BlockSpec(lb.shape, const2),
            pl.BlockSpec(w.shape, const2),
            pl.BlockSpec(g.shape, const2),
        ],
        out_specs=pl.BlockSpec((T, D_MODEL), lambda b, t: (b * nt + t, 0)),
        out_shape=jax.ShapeDtypeStruct((m, D_MODEL), F32),
        scratch_shapes=[
            pltpu.VMEM((T + CONV_HALO, D_CONV), F32),
            pltpu.VMEM((T, D_CONV), F32),
            pltpu.VMEM((SUBLANES - 1, T + CONV_HALO - SUBLANES, LANES), F32),
        ],
        compiler_params=pltpu.CompilerParams(
            dimension_semantics=("arbitrary", "arbitrary"), vmem_limit_bytes=VMEM_LIMIT),
        name="conv1",
    )(h, z, x2, cw, cb, lg, lb, w, g)


def _pad_lanes(v, width=LANES):
    return jnp.pad(v, (0, width - v.shape[0]))


def _even_layer(x2, bsz, seq, g_pre, w_in, conv_w, conv_b, dt_bias, a_log, d_skip, fgate_b,
                ssd_norm, w_out, g_post):
    d_xbc = D_MODEL + 2 * SSD_GROUPS * D_STATE
    o_z, o_xbc = 0, 2 * D_MODEL
    o_dt = o_xbc + d_xbc
    o_q = o_dt + SSD_HEADS
    o_k = o_q + D_MODEL
    o_v = o_k + D_MODEL
    o_f = o_v + D_MODEL
    q_scale = HEADDIM ** -0.5 * LOG2E
    w_main = jnp.concatenate(
        [w_in[:, o_z:o_dt], w_in[:, o_q:o_k] * q_scale, w_in[:, o_k:o_v]],
        axis=1).astype(BF16)
    wvt = w_in[:, o_v:o_f].T.astype(BF16)
    w_small = jnp.concatenate(
        [w_in[:, o_dt:o_q], w_in[:, o_f:o_f + FOX_HEADS],
         jnp.zeros((D_MODEL, LANES - SSD_HEADS - FOX_HEADS), F32)], axis=1)
    wsh = w_small.astype(BF16)
    wsl = (w_small - wsh.astype(F32)).astype(BF16)

    proj, vt, sm, smt = _inproj0(x2, g_pre[None, :], w_main, wvt, wsh, wsl,
                                 tm=TILES.proj_rows, tn=TILES.proj_cols)

    zeros16 = jnp.zeros((SSD_HEADS,), F32)
    prow = jnp.stack([
        _pad_lanes(dt_bias),
        _pad_lanes(a_log),
        _pad_lanes(jnp.concatenate([zeros16, fgate_b])),
    ] + [jnp.zeros((LANES,), F32)] * 5)
    pcol = jnp.stack([jnp.broadcast_to(dt_bias[:, None], (SSD_HEADS, LANES)),
                      jnp.broadcast_to(a_log[:, None], (SSD_HEADS, LANES))])
    dskip_e = jnp.repeat(d_skip, HEADDIM)[None, :]
    head_of_lane = jnp.arange(D_MODEL) // HEADDIM
    e64 = (jnp.arange(LANES)[:, None] == head_of_lane[None, :]).astype(BF16)

    y, c = _ssd(proj, sm, smt, conv_w, conv_b[None, :], prow, pcol, dskip_e,
                ssd_norm[None, :], e64, bsz=bsz, seq=seq, T=TILES.ssd_rows)

    ct = c[:, SSD_HEADS:SSD_HEADS + FOX_HEADS].T.reshape(FOX_HEADS // 2, 2, -1)
    c8 = jnp.pad(ct, ((0, 0), (0, 6), (0, 0)))

    o = _fox(proj, vt, c8, c, bsz=bsz, seq=seq, tq=TILES.fox_q, tk=TILES.fox_kv,
             q_cb0=4 * CB, k_cb0=5 * CB, c_lane0=SSD_HEADS)
    return _outproj0(y, o, proj, x2, w_out.astype(BF16), g_post[None, :], tm=TILES.out_rows)


def _odd_layer(x2, bsz, seq, g_pre, w_in, conv_w, conv_b, ln_g, ln_b, w_out, g_post):
    seg = TILES.glu_seg
    cols = []
    for j in range(D_CONV // seg):
        for part in range(3):
            cols.append(w_in[:, part * D_CONV + j * seg:part * D_CONV + (j + 1) * seg])
    w = jnp.concatenate(cols, axis=1).astype(BF16)
    h, z = _inproj1(x2, g_pre[None, :], w, tm=TILES.proj_rows, seg=seg)
    return _conv1(h, z, x2, conv_w, conv_b[None, :], ln_g[None, :], ln_b[None, :],
                  w_out.astype(BF16), g_post[None, :], bsz=bsz, seq=seq, T=TILES.conv_rows)


def kernel(x, e_norm_pre, e_w_in, e_conv_w, e_conv_b, e_dt_bias, e_a_log, e_d_skip, e_fgate_b,
           e_ssd_norm, e_w_out, e_norm_post, o_norm_pre, o_w_in, o_conv_w, o_conv_b, o_ln_g,
           o_ln_b, o_w_out, o_norm_post):
    bsz, seq, d = x.shape
    x2 = x.reshape(bsz * seq, d)
    depth = e_w_in.shape[0] + o_w_in.shape[0]
    for layer in range(depth):
        i = layer // 2
        if layer % 2 == 0:
            x2 = _even_layer(x2, bsz, seq, e_norm_pre[i], e_w_in[i], e_conv_w[i], e_conv_b[i],
                             e_dt_bias[i], e_a_log[i], e_d_skip[i], e_fgate_b[i],
                             e_ssd_norm[i], e_w_out[i], e_norm_post[i])
        else:
            x2 = _odd_layer(x2, bsz, seq, o_norm_pre[i], o_w_in[i], o_conv_w[i], o_conv_b[i],
                            o_ln_g[i], o_ln_b[i], o_w_out[i], o_norm_post[i])
    return x2.reshape(bsz, seq, d)
```

```python
import functools
import math
from typing import NamedTuple

import jax
import jax.numpy as jnp
from jax import lax
from jax.experimental import pallas as pl
from jax.experimental.pallas import tpu as pltpu

F32 = jnp.float32
BF16 = jnp.bfloat16

LANES = 128
SUBLANES = 8
D_MODEL = 1024
SSD_HEADS = 16
SSD_GROUPS = 4
HEADDIM = 64
D_STATE = 128
CHUNK = 128
SSD_CONV = 4
FOX_HEADS = 16
D_CONV = 2 * D_MODEL
CONV_WIDTH = 31
EPS = 1e-6
NEG_BIG = -1e30
LOG2E = 1.4426950408889634
STRIP = 256
ONES_ROWS = 16
CONV_HALO = 32
LOOKAHEAD = 4
FOX_UNROLL = 4
VMEM_LIMIT = 56 * 1024 * 1024

CB = D_MODEL // LANES


class _Tiles(NamedTuple):
    proj_rows: int = 1024
    proj_cols: int = 1024
    glu_seg: int = 512
    ssd_rows: int = 512
    fox_q: int = 2048
    fox_kv: int = 512
    out_rows: int = 1024
    conv_rows: int = 512


TILES = _Tiles()


def _dot(a, b):
    return jnp.dot(a, b, preferred_element_type=F32)


def _dot_nt(a, b):
    return lax.dot_general(a, b, (((1,), (1,)), ((), ())), preferred_element_type=F32)


def _dot_tn(a, b):
    return lax.dot_general(a, b, (((0,), (0,)), ((), ())), preferred_element_type=F32)


def _split(a, n):
    pieces = []
    r = a
    for _ in range(n):
        p = r.astype(BF16)
        pieces.append(p)
        r = r - p.astype(F32)
    return pieces


def _dot_split_lhs(a, m_bf16, n):
    out = None
    for p in _split(a, n):
        t = _dot(p, m_bf16)
        out = t if out is None else out + t
    return out


def _dot_split_rhs(m_bf16, a, n):
    out = None
    for p in _split(a, n):
        t = _dot(m_bf16, p)
        out = t if out is None else out + t
    return out


def _sigmoid(x):
    return 1.0 / (1.0 + jnp.exp(-x))


def _silu(x):
    return x * _sigmoid(x)


def _softplus(x):
    return jnp.maximum(x, 0.0) + jnp.log(1.0 + jnp.exp(-jnp.abs(x)))


def _log_sigmoid(x):
    return jnp.minimum(x, 0.0) - jnp.log(1.0 + jnp.exp(-jnp.abs(x)))


def _cat_blocks(ref, lo, hi):
    return jnp.concatenate([ref[c] for c in range(lo, hi)], axis=-1)


def _inproj0_kernel(x_ref, g_ref, w_ref, wvt_ref, wsh_ref, wsl_ref,
                    proj_ref, vt_ref, sm_ref, smt_ref, xn_scr, *, ncb):
    j = pl.program_id(1)

    @pl.when(j == 0)
    def _():
        x = x_ref[...]
        ms = jnp.mean(x * x, axis=-1, keepdims=True)
        xn = x * lax.rsqrt(ms + EPS) * g_ref[...]
        hi = xn.astype(BF16)
        lo = (xn - hi.astype(F32)).astype(BF16)
        xn_scr[...] = hi
        small = _dot(hi, wsh_ref[...]) + _dot(lo, wsh_ref[...]) + _dot(hi, wsl_ref[...])
        sm_ref[...] = small
        smt_ref[...] = small.T
        vt = _dot_nt(wvt_ref[...], hi)
        for c in range(CB):
            vt_ref[c] = vt[c * LANES:(c + 1) * LANES, :].astype(BF16)

    acc = _dot(xn_scr[...], w_ref[...])
    for c in range(ncb):
        proj_ref[c] = acc[:, c * LANES:(c + 1) * LANES].astype(BF16)


def _inproj0(x2, g, w_main, wvt, wsh, wsl, *, tm, tn):
    m = x2.shape[0]
    ncol = w_main.shape[1] // tn
    ncb = tn // LANES
    kern = functools.partial(_inproj0_kernel, ncb=ncb)
    return pl.pallas_call(
        kern,
        grid=(m // tm, ncol),
        in_specs=[
            pl.BlockSpec((tm, D_MODEL), lambda i, j: (i, 0)),
            pl.BlockSpec((1, D_MODEL), lambda i, j: (0, 0)),
            pl.BlockSpec((D_MODEL, tn), lambda i, j: (0, j)),
            pl.BlockSpec((D_MODEL, D_MODEL), lambda i, j: (0, 0)),
            pl.BlockSpec((D_MODEL, LANES), lambda i, j: (0, 0)),
            pl.BlockSpec((D_MODEL, LANES), lambda i, j: (0, 0)),
        ],
        out_specs=[
            pl.BlockSpec((ncb, tm, LANES), lambda i, j: (j, i, 0)),
            pl.BlockSpec((CB, LANES, tm), lambda i, j: (0, 0, i)),
            pl.BlockSpec((tm, LANES), lambda i, j: (i, 0)),
            pl.BlockSpec((LANES, tm), lambda i, j: (0, i)),
        ],
        out_shape=[
            jax.ShapeDtypeStruct((w_main.shape[1] // LANES, m, LANES), BF16),
            jax.ShapeDtypeStruct((CB, LANES, m), BF16),
            jax.ShapeDtypeStruct((m, LANES), F32),
            jax.ShapeDtypeStruct((LANES, m), F32),
        ],
        scratch_shapes=[pltpu.VMEM((tm, D_MODEL), BF16)],
        compiler_params=pltpu.CompilerParams(
            dimension_semantics=("arbitrary", "arbitrary"), vmem_limit_bytes=VMEM_LIMIT),
        name="inproj0",
    )(x2, g, w_main, wvt, wsh, wsl)


def _ssd_kernel(xbc_ref, z_ref, sm_ref, smt_ref, convw_ref, convb_ref, prow_ref, pcol_ref,
                dskip_ref, norm_ref, e64_ref,
                y_ref, c_ref,
                h_scr, ext_scr, xc_scr, ccarry_scr, sh_scr, *, T):
    t = pl.program_id(1)
    nxc = 2 * CB

    @pl.when(t == 0)
    def _():
        h_scr[...] = jnp.zeros_like(h_scr)
        ext_scr[0:8, :] = jnp.zeros((8, nxc * LANES), F32)
        ccarry_scr[...] = jnp.zeros_like(ccarry_scr)

    for c in range(nxc):
        ext_scr[8:8 + T, c * LANES:(c + 1) * LANES] = xbc_ref[c].astype(F32)
    for c in range(nxc):
        cs_ = slice(c * LANES, (c + 1) * LANES)
        for k in range(SSD_CONV - 1):
            off = SUBLANES - (SSD_CONV - 1) + k
            sh_scr[k] = ext_scr[off:off + T, cs_]
        acc = convb_ref[:, cs_] + (ext_scr[SUBLANES:SUBLANES + T, cs_]
                                   * convw_ref[SSD_CONV - 1:SSD_CONV, cs_])
        for k in range(SSD_CONV - 1):
            acc = acc + sh_scr[k] * convw_ref[k:k + 1, cs_]
        xc_scr[:, cs_] = _silu(acc)
    ext_scr[0:8, :] = ext_scr[T:T + 8, :]

    dtb_row = prow_ref[0:1, :]
    a_row = -jnp.exp(prow_ref[1:2, :])
    fb_row = prow_ref[2:3, :]
    dtb_col = pcol_ref[0]
    a_col = -jnp.exp(pcol_ref[1])

    rt = lax.broadcasted_iota(jnp.int32, (T, T), 0)
    ct = lax.broadcasted_iota(jnp.int32, (T, T), 1)
    ltri_t = jnp.where(rt >= ct, 1.0, 0.0).astype(BF16)
    logf = _log_sigmoid(sm_ref[...] + fb_row)
    ctile = _dot_split_rhs(ltri_t, logf, 3) + ccarry_scr[...]
    c_ref[...] = ctile
    ccarry_scr[...] = ctile[T - 1:T, :]

    ri = lax.broadcasted_iota(jnp.int32, (CHUNK, CHUNK), 0)
    ci = lax.broadcasted_iota(jnp.int32, (CHUNK, CHUNK), 1)
    causal = ri >= ci
    ltri = jnp.where(causal, 1.0, 0.0).astype(BF16)
    utri = jnp.where(ri <= ci, 1.0, 0.0).astype(BF16)
    lane = lax.broadcasted_iota(jnp.int32, (CHUNK, LANES), 1)
    first_head = lane < HEADDIM
    e64 = e64_ref[...]

    for ck in range(T // CHUNK):
        r = slice(ck * CHUNK, (ck + 1) * CHUNK)
        dt = _softplus(sm_ref[r, :] + dtb_row)
        cs = _dot_split_rhs(ltri, dt * a_row, 3)
        dtt = _softplus(smt_ref[0:SSD_HEADS, r] + dtb_col)
        cst = _dot_split_lhs(dtt * a_col, utri, 3)
        cs_e = _dot_split_lhs(cs, e64, 3)
        dt_e = _dot_split_lhs(dt, e64, 3)
        ecs_e = jnp.exp(cs_e)
        dte_e = jnp.exp(cs_e[CHUNK - 1:CHUNK, :] - cs_e)

        for g in range(SSD_GROUPS):
            bsl = slice(D_MODEL + g * D_STATE, D_MODEL + (g + 1) * D_STATE)
            csl = slice(D_MODEL + (SSD_GROUPS + g) * D_STATE,
                        D_MODEL + (SSD_GROUPS + g + 1) * D_STATE)
            bg = xc_scr[r, bsl].astype(BF16)
            cg = xc_scr[r, csl].astype(BF16)
            cb_ = _dot_nt(cg, bg)
            ys = []
            for pp in range(2):
                p = 2 * g + pp
                ps = slice(p * LANES, (p + 1) * LANES)
                x = xc_scr[r, ps]
                xd = x * dt_e[:, ps]
                xd_b = xd.astype(BF16)
                yd = []
                for hh in range(2):
                    h = 2 * p + hh
                    colb = jnp.broadcast_to(cs[:, h:h + 1], (CHUNK, CHUNK))
                    rowb = jnp.broadcast_to(cst[h:h + 1, :], (CHUNK, CHUNK))
                    lm = jnp.where(causal, jnp.exp(colb - rowb), 0.0)
                    yd.append(_dot((cb_ * lm).astype(BF16), xd_b))
                ydiag = jnp.where(first_head, yd[0], yd[1])
                hin = h_scr[:, ps]
                yoff = _dot(cg, hin.astype(BF16)) * ecs_e[:, ps]
                snew = _dot_tn(bg, (xd * dte_e[:, ps]).astype(BF16))
                h_scr[:, ps] = hin * ecs_e[CHUNK - 1:CHUNK, ps] + snew
                yp = ydiag + yoff + x * dskip_ref[:, ps]
                yp = yp * _silu(z_ref[p, r, :].astype(F32))
                ys.append(yp)
            ss = (jnp.sum(ys[0] * ys[0], axis=-1, keepdims=True)
                  + jnp.sum(ys[1] * ys[1], axis=-1, keepdims=True))
            scale = lax.rsqrt(ss * (1.0 / (2 * LANES)) + EPS)
            for pp in range(2):
                p = 2 * g + pp
                ps = slice(p * LANES, (p + 1) * LANES)
                y_ref[p, r, :] = (ys[pp] * scale * norm_ref[:, ps]).astype(BF16)


def _ssd(proj, sm, smt, convw, convb, prow, pcol, dskip_e, norm, e64, *, bsz, seq, T):
    m = bsz * seq
    nt = seq // T
    kern = functools.partial(_ssd_kernel, T=T)
    const2 = lambda b, t: (0, 0)
    return pl.pallas_call(
        kern,
        grid=(bsz, nt),
        in_specs=[
            pl.BlockSpec((2 * CB, T, LANES), lambda b, t: (1, b * nt + t, 0)),
            pl.BlockSpec((CB, T, LANES), lambda b, t: (0, b * nt + t, 0)),
            pl.BlockSpec((T, LANES), lambda b, t: (b * nt + t, 0)),
            pl.BlockSpec((LANES, T), lambda b, t: (0, b * nt + t)),
            pl.BlockSpec(convw.shape, const2),
            pl.BlockSpec(convb.shape, const2),
            pl.BlockSpec(prow.shape, const2),
            pl.BlockSpec(pcol.shape, lambda b, t: (0, 0, 0)),
            pl.BlockSpec(dskip_e.shape, const2),
            pl.BlockSpec(norm.shape, const2),
            pl.BlockSpec(e64.shape, const2),
        ],
        out_specs=[
            pl.BlockSpec((CB, T, LANES), lambda b, t: (0, b * nt + t, 0)),
            pl.BlockSpec((T, LANES), lambda b, t: (b * nt + t, 0)),
        ],
        out_shape=[
            jax.ShapeDtypeStruct((CB, m, LANES), BF16),
            jax.ShapeDtypeStruct((m, LANES), F32),
        ],
        scratch_shapes=[
            pltpu.VMEM((D_STATE, D_MODEL), F32),
            pltpu.VMEM((T + 8, 2 * D_MODEL), F32),
            pltpu.VMEM((T, 2 * D_MODEL), F32),
            pltpu.VMEM((1, LANES), F32),
            pltpu.VMEM((SSD_CONV - 1, T, LANES), F32),
        ],
        compiler_params=pltpu.CompilerParams(
            dimension_semantics=("arbitrary", "arbitrary"), vmem_limit_bytes=VMEM_LIMIT),
        name="ssd",
    )(proj, proj, sm, smt, convw, convb, prow, pcol, dskip_e, norm, e64)


def _fox_kernel(q_ref, k_ref, vt_ref, cq_ref, c_ref, o_ref, qh_scr, m_scr, acc_scr, *,
                tq, tk, c_lane0):
    pair = pl.program_id(1)
    i = pl.program_id(2)
    nstrip = tq // STRIP
    lane_k = lax.broadcasted_iota(jnp.int32, (tk, LANES), 1)
    lane = lax.broadcasted_iota(jnp.int32, (tq, LANES), 1)
    q = q_ref[...]
    zero = jnp.zeros_like(q)
    qh_scr[0] = jnp.where(lane < HEADDIM, q, zero)
    qh_scr[1] = jnp.where(lane >= HEADDIM, q, zero)
    m_scr[...] = jnp.full_like(m_scr, NEG_BIG)
    acc_scr[...] = jnp.zeros_like(acc_scr)
    ones = jnp.ones((ONES_ROWS, tk), BF16)

    def scores(hh, st, nkv, k):
        return _dot_nt(k[0:nkv], qh_scr[hh, st * STRIP:(st + 1) * STRIP, :])

    def softmax_pv(hh, st, nkv, s, ck2, vta, koff, partial):
        cols = slice(st * STRIP, (st + 1) * STRIP)
        t = s - ck2[0:nkv]
        vta = vta[:, 0:nkv]
        if partial:
            kv_pos = koff + lax.broadcasted_iota(jnp.int32, (nkv, STRIP), 0)
            q_pos = i * tq + st * STRIP + lax.broadcasted_iota(jnp.int32, (nkv, STRIP), 1)
            t = jnp.where(kv_pos <= q_pos, t, NEG_BIG)
        cq2 = cq_ref[hh:hh + 1, cols] * LOG2E
        m_prev = m_scr[hh, :, cols]
        m_new = jnp.maximum(m_prev, jnp.max(t, axis=0, keepdims=True) + cq2)
        alpha = jnp.exp2(m_prev - m_new)
        p = jnp.exp2((t - (m_new - cq2)).astype(BF16))
        acc_scr[hh, :, cols] = acc_scr[hh, :, cols] * alpha + _dot(vta, p)
        m_scr[hh, :, cols] = m_new

    def strips_of(diag):
        items = []
        for hh in range(2):
            for st in range(nstrip):
                nkv, partial = tk, False
                if diag is not None:
                    kv_lo, q_lo = diag * tk, st * STRIP
                    if kv_lo > q_lo + STRIP - 1:
                        continue
                    nkv = min(tk, q_lo + STRIP - kv_lo)
                    partial = kv_lo + nkv - 1 > q_lo
                items.append((hh, st, nkv, partial))
        return items

    def blocks(j0, diags):
        koffs, ks, ck2s, vtas, items = [], [], [], [], []
        for b, diag in enumerate(diags):
            koff = pl.multiple_of((j0 + b) * tk, tk)
            cblk = c_ref[pl.ds(koff, tk), :]
            ck2, vta = [], []
            for hh in range(2):
                sel = lane_k == (c_lane0 + 2 * pair + hh)
                ck2.append(jnp.sum(jnp.where(sel, cblk, 0.0), axis=1, keepdims=True) * LOG2E)
                vta.append(jnp.concatenate(
                    [vt_ref[hh * HEADDIM:(hh + 1) * HEADDIM, pl.ds(koff, tk)], ones], axis=0))
            koffs.append(koff)
            ks.append(k_ref[pl.ds(koff, tk), :])
            ck2s.append(ck2)
            vtas.append(vta)
            items += [(b,) + it for it in strips_of(diag)]
        pending = [scores(hh, st, nkv, ks[b]) for b, hh, st, nkv, _ in items[:LOOKAHEAD]]
        for n, (b, hh, st, nkv, partial) in enumerate(items):
            if n + LOOKAHEAD < len(items):
                nb, nhh, nst, nnkv, _ = items[n + LOOKAHEAD]
                pending.append(scores(nhh, nst, nnkv, ks[nb]))
            softmax_pv(hh, st, nkv, pending.pop(0), ck2s[b][hh], vtas[b][hh], koffs[b], partial)

    ndiag = tq // tk
    nfull = i * ndiag
    unroll = math.gcd(ndiag, FOX_UNROLL)

    def body(j, carry):
        blocks(j * unroll, [None] * unroll)
        return carry

    lax.fori_loop(0, nfull // unroll, body, 0)
    blocks(nfull, list(range(ndiag)))

    outs = []
    for hh in range(2):
        acc = acc_scr[hh]
        outs.append(acc[0:HEADDIM, :] * (1.0 / acc[HEADDIM:HEADDIM + 1, :]))
    o_ref[...] = jnp.concatenate(outs, axis=0).T.astype(BF16)


def _fox(proj, vt, c8, c, *, bsz, seq, tq, tk, q_cb0, k_cb0, c_lane0):
    m = bsz * seq
    nq = seq // tq
    npair = FOX_HEADS // 2
    kern = functools.partial(_fox_kernel, tq=tq, tk=tk, c_lane0=c_lane0)
    return pl.pallas_call(
        kern,
        grid=(bsz, npair, nq),
        in_specs=[
            pl.BlockSpec((None, tq, LANES), lambda b, p, i: (q_cb0 + p, b * nq + i, 0)),
            pl.BlockSpec((None, seq, LANES), lambda b, p, i: (k_cb0 + p, b, 0)),
            pl.BlockSpec((None, LANES, seq), lambda b, p, i: (p, 0, b)),
            pl.BlockSpec((None, 8, tq), lambda b, p, i: (p, 0, b * nq + i)),
            pl.BlockSpec((seq, LANES), lambda b, p, i: (b, 0)),
        ],
        out_specs=pl.BlockSpec((None, tq, LANES), lambda b, p, i: (p, b * nq + i, 0)),
        out_shape=jax.ShapeDtypeStruct((npair, m, LANES), BF16),
        scratch_shapes=[
            pltpu.VMEM((2, tq, LANES), BF16),
            pltpu.VMEM((2, 1, tq), F32),
            pltpu.VMEM((2, HEADDIM + ONES_ROWS, tq), F32),
        ],
        compiler_params=pltpu.CompilerParams(
            dimension_semantics=("arbitrary", "arbitrary", "arbitrary"),
            vmem_limit_bytes=VMEM_LIMIT),
        name="fox",
    )(proj, proj, vt, c8, c)


def _outproj0_kernel(y_ref, o_ref, z_ref, x_ref, w_ref, g_ref, out_ref):
    y = _cat_blocks(y_ref, 0, CB)
    o = _cat_blocks(o_ref, 0, CB).astype(F32)
    z = _cat_blocks(z_ref, 0, CB).astype(F32)
    og = (o * _silu(z)).astype(BF16)
    acc = _dot(y, w_ref[0:D_MODEL, :]) + _dot(og, w_ref[D_MODEL:2 * D_MODEL, :])
    ms = jnp.mean(acc * acc, axis=-1, keepdims=True)
    out_ref[...] = x_ref[...] + acc * lax.rsqrt(ms + EPS) * g_ref[...]


def _outproj0(y, o, proj, x2, w, g, *, tm):
    m = x2.shape[0]
    return pl.pallas_call(
        _outproj0_kernel,
        grid=(m // tm,),
        in_specs=[
            pl.BlockSpec((CB, tm, LANES), lambda i: (0, i, 0)),
            pl.BlockSpec((CB, tm, LANES), lambda i: (0, i, 0)),
            pl.BlockSpec((CB, tm, LANES), lambda i: (1, i, 0)),
            pl.BlockSpec((tm, D_MODEL), lambda i: (i, 0)),
            pl.BlockSpec(w.shape, lambda i: (0, 0)),
            pl.BlockSpec((1, D_MODEL), lambda i: (0, 0)),
        ],
        out_specs=pl.BlockSpec((tm, D_MODEL), lambda i: (i, 0)),
        out_shape=jax.ShapeDtypeStruct((m, D_MODEL), F32),
        compiler_params=pltpu.CompilerParams(
            dimension_semantics=("arbitrary",), vmem_limit_bytes=VMEM_LIMIT),
        name="outproj0",
    )(y, o, proj, x2, w, g)


def _inproj1_kernel(x_ref, g_ref, w_ref, h_ref, z_ref, xn_scr, *, seg):
    j = pl.program_id(1)

    @pl.when(j == 0)
    def _():
        x = x_ref[...]
        ms = jnp.mean(x * x, axis=-1, keepdims=True)
        xn_scr[...] = (x * lax.rsqrt(ms + EPS) * g_ref[...]).astype(BF16)

    acc = _dot(xn_scr[...], w_ref[...])
    hv = acc[:, 0:seg] * _sigmoid(acc[:, seg:2 * seg])
    for c in range(seg // LANES):
        h_ref[c] = hv[:, c * LANES:(c + 1) * LANES].astype(BF16)
        z_ref[c] = acc[:, 2 * seg + c * LANES:2 * seg + (c + 1) * LANES].astype(BF16)


def _inproj1(x2, g, w, *, tm, seg):
    m = x2.shape[0]
    tn = 3 * seg
    ncol = w.shape[1] // tn
    scb = seg // LANES
    kern = functools.partial(_inproj1_kernel, seg=seg)
    return pl.pallas_call(
        kern,
        grid=(m // tm, ncol),
        in_specs=[
            pl.BlockSpec((tm, D_MODEL), lambda i, j: (i, 0)),
            pl.BlockSpec((1, D_MODEL), lambda i, j: (0, 0)),
            pl.BlockSpec((D_MODEL, tn), lambda i, j: (0, j)),
        ],
        out_specs=[
            pl.BlockSpec((scb, tm, LANES), lambda i, j: (j, i, 0)),
            pl.BlockSpec((scb, tm, LANES), lambda i, j: (j, i, 0)),
        ],
        out_shape=[
            jax.ShapeDtypeStruct((D_CONV // LANES, m, LANES), BF16),
            jax.ShapeDtypeStruct((D_CONV // LANES, m, LANES), BF16),
        ],
        scratch_shapes=[pltpu.VMEM((tm, D_MODEL), BF16)],
        compiler_params=pltpu.CompilerParams(
            dimension_semantics=("arbitrary", "arbitrary"), vmem_limit_bytes=VMEM_LIMIT),
        name="inproj1",
    )(x2, g, w)


def _conv1_kernel(h_ref, z_ref, x_ref, cw_ref, cb_ref, lg_ref, lb_ref, w_ref, g_ref,
                  out_ref, ext_scr, hc_scr, sh_scr, *, T):
    t = pl.program_id(1)
    ncb = D_CONV // LANES
    halo = CONV_HALO

    @pl.when(t == 0)
    def _():
        ext_scr[0:halo, :] = jnp.zeros((halo, D_CONV), F32)

    for c in range(ncb):
        ext_scr[halo:halo + T, c * LANES:(c + 1) * LANES] = h_ref[c].astype(F32)
    span = T + halo - SUBLANES
    for c in range(ncb):
        cs_ = slice(c * LANES, (c + 1) * LANES)
        for r in range(1, SUBLANES):
            sh_scr[r - 1] = ext_scr[r:r + span, cs_]
        acc = cb_ref[:, cs_]
        for k in range(CONV_WIDTH):
            base, r = divmod(halo - (CONV_WIDTH - 1) + k, SUBLANES)
            base *= SUBLANES
            if r == 0:
                src = ext_scr[base:base + T, cs_]
            else:
                src = sh_scr[r - 1, base:base + T, :]
            acc = acc + src * cw_ref[k:k + 1, cs_]
        hc_scr[:, cs_] = acc
    ext_scr[0:halo, :] = ext_scr[T:T + halo, :]

    hc = hc_scr[...]
    mu = jnp.mean(hc, axis=-1, keepdims=True)
    xc = hc - mu
    var = jnp.mean(xc * xc, axis=-1, keepdims=True)
    hn = xc * lax.rsqrt(var + EPS) * lg_ref[...] + lb_ref[...]
    z = _cat_blocks(z_ref, 0, ncb).astype(F32)
    hg = (_silu(hn) * _silu(z)).astype(BF16)
    acc = _dot(hg, w_ref[...])
    ms = jnp.mean(acc * acc, axis=-1, keepdims=True)
    out_ref[...] = x_ref[...] + acc * lax.rsqrt(ms + EPS) * g_ref[...]


def _conv1(h, z, x2, cw, cb, lg, lb, w, g, *, bsz, seq, T):
    m = bsz * seq
    nt = seq // T
    ncb = D_CONV // LANES
    kern = functools.partial(_conv1_kernel, T=T)
    const2 = lambda b, t: (0, 0)
    return pl.pallas_call(
        kern,
        grid=(bsz, nt),
        in_specs=[
            pl.BlockSpec((ncb, T, LANES), lambda b, t: (0, b * nt + t, 0)),
            pl.BlockSpec((ncb, T, LANES), lambda b, t: (0, b * nt + t, 0)),
            pl.BlockSpec((T, D_MODEL), lambda b, t: (b * nt + t, 0)),
            pl.BlockSpec(cw.shape, const2),
            pl.BlockSpec(cb.shape, const2),
            pl.BlockSpec(lg.shape, const2),
            pl.BlockSpec(lb.shape, const2),
            pl.BlockSpec(w.shape, const2),
            pl.BlockSpec(g.shape, const2),
        ],
        out_specs=pl.BlockSpec((T, D_MODEL), lambda b, t: (b * nt + t, 0)),
        out_shape=jax.ShapeDtypeStruct((m, D_MODEL), F32),
        scratch_shapes=[
            pltpu.VMEM((T + CONV_HALO, D_CONV), F32),
            pltpu.VMEM((T, D_CONV), F32),
            pltpu.VMEM((SUBLANES - 1, T + CONV_HALO - SUBLANES, LANES), F32),
        ],
        compiler_params=pltpu.CompilerParams(
            dimension_semantics=("arbitrary", "arbitrary"), vmem_limit_bytes=VMEM_LIMIT),
        name="conv1",
    )(h, z, x2, cw, cb, lg, lb, w, g)


def _pad_lanes(v, width=LANES):
    return jnp.pad(v, (0, width - v.shape[0]))


def _even_layer(x2, bsz, seq, g_pre, w_in, conv_w, conv_b, dt_bias, a_log, d_skip, fgate_b,
                ssd_norm, w_out, g_post):
    d_xbc = D_MODEL + 2 * SSD_GROUPS * D_STATE
    o_z, o_xbc = 0, 2 * D_MODEL
    o_dt = o_xbc + d_xbc
    o_q = o_dt + SSD_HEADS
    o_k = o_q + D_MODEL
    o_v = o_k + D_MODEL
    o_f = o_v + D_MODEL
    q_scale = HEADDIM ** -0.5 * LOG2E
    w_main = jnp.concatenate(
        [w_in[:, o_z:o_dt], w_in[:, o_q:o_k] * q_scale, w_in[:, o_k:o_v]],
        axis=1).astype(BF16)
    wvt = w_in[:, o_v:o_f].T.astype(BF16)
    w_small = jnp.concatenate(
        [w_in[:, o_dt:o_q], w_in[:, o_f:o_f + FOX_HEADS],
         jnp.zeros((D_MODEL, LANES - SSD_HEADS - FOX_HEADS), F32)], axis=1)
    wsh = w_small.astype(BF16)
    wsl = (w_small - wsh.astype(F32)).astype(BF16)

    proj, vt, sm, smt = _inproj0(x2, g_pre[None, :], w_main, wvt, wsh, wsl,
                                 tm=TILES.proj_rows, tn=TILES.proj_cols)

    zeros16 = jnp.zeros((SSD_HEADS,), F32)
    prow = jnp.stack([
        _pad_lanes(dt_bias),
        _pad_lanes(a_log),
        _pad_lanes(jnp.concatenate([zeros16, fgate_b])),
    ] + [jnp.zeros((LANES,), F32)] * 5)
    pcol = jnp.stack([jnp.broadcast_to(dt_bias[:, None], (SSD_HEADS, LANES)),
                      jnp.broadcast_to(a_log[:, None], (SSD_HEADS, LANES))])
    dskip_e = jnp.repeat(d_skip, HEADDIM)[None, :]
    head_of_lane = jnp.arange(D_MODEL) // HEADDIM
    e64 = (jnp.arange(LANES)[:, None] == head_of_lane[None, :]).astype(BF16)

    y, c = _ssd(proj, sm, smt, conv_w, conv_b[None, :], prow, pcol, dskip_e,
                ssd_norm[None, :], e64, bsz=bsz, seq=seq, T=TILES.ssd_rows)

    ct = c[:, SSD_HEADS:SSD_HEADS + FOX_HEADS].T.reshape(FOX_HEADS // 2, 2, -1)
    c8 = jnp.pad(ct, ((0, 0), (0, 6), (0, 0)))

    o = _fox(proj, vt, c8, c, bsz=bsz, seq=seq, tq=TILES.fox_q, tk=TILES.fox_kv,
             q_cb0=4 * CB, k_cb0=5 * CB, c_lane0=SSD_HEADS)
    return _outproj0(y, o, proj, x2, w_out.astype(BF16), g_post[None, :], tm=TILES.out_rows)


def _odd_layer(x2, bsz, seq, g_pre, w_in, conv_w, conv_b, ln_g, ln_b, w_out, g_post):
    seg = TILES.glu_seg
    cols = []
    for j in range(D_CONV // seg):
        for part in range(3):
            cols.append(w_in[:, part * D_CONV + j * seg:part * D_CONV + (j + 1) * seg])
    w = jnp.concatenate(cols, axis=1).astype(BF16)
    h, z = _inproj1(x2, g_pre[None, :], w, tm=TILES.proj_rows, seg=seg)
    return _conv1(h, z, x2, conv_w, conv_b[None, :], ln_g[None, :], ln_b[None, :],
                  w_out.astype(BF16), g_post[None, :], bsz=bsz, seq=seq, T=TILES.conv_rows)


def kernel(x, e_norm_pre, e_w_in, e_conv_w, e_conv_b, e_dt_bias, e_a_log, e_d_skip, e_fgate_b,
           e_ssd_norm, e_w_out, e_norm_post, o_norm_pre, o_w_in, o_conv_w, o_conv_b, o_ln_g,
           o_ln_b, o_w_out, o_norm_post):
    bsz, seq, d = x.shape
    x2 = x.reshape(bsz * seq, d)
    depth = e_w_in.shape[0] + o_w_in.shape[0]
    for layer in range(depth):
        i = layer // 2
        if layer % 2 == 0:
            x2 = _even_layer(x2, bsz, seq, e_norm_pre[i], e_w_in[i], e_conv_w[i], e_conv_b[i],
                             e_dt_bias[i], e_a_log[i], e_d_skip[i], e_fgate_b[i],
                             e_ssd_norm[i], e_w_out[i], e_norm_post[i])
        else:
            x2 = _odd_layer(x2, bsz, seq, o_norm_pre[i], o_w_in[i], o_conv_w[i], o_conv_b[i],
                            o_ln_g[i], o_ln_b[i], o_w_out[i], o_norm_post[i])
    return x2.reshape(bsz, seq, d)
```

```python
import functools
import math
from typing import NamedTuple

import jax
import jax.numpy as jnp
from jax import lax
from jax.experimental import pallas as pl
from jax.experimental.pallas import tpu as pltpu

F32 = jnp.float32
BF16 = jnp.bfloat16

LANES = 128
SUBLANES = 8
D_MODEL = 1024
SSD_HEADS = 16
SSD_GROUPS = 4
HEADDIM = 64
D_STATE = 128
CHUNK = 128
SSD_CONV = 4
FOX_HEADS = 16
D_CONV = 2 * D_MODEL
CONV_WIDTH = 31
EPS = 1e-6
NEG_BIG = -1e30
LOG2E = 1.4426950408889634
STRIP = 256
ONES_ROWS = 16
CONV_HALO = 32
LOOKAHEAD = 4
FOX_UNROLL = 4
VMEM_LIMIT = 56 * 1024 * 1024

CB = D_MODEL // LANES


class _Tiles(NamedTuple):
    proj_rows: int = 1024
    proj_cols: int = 2048
    glu_seg: int = 512
    ssd_rows: int = 256
    fox_q: int = 2048
    fox_kv: int = 512
    out_rows: int = 1024
    conv_rows: int = 512


TILES = _Tiles()


def _dot(a, b):
    return jnp.dot(a, b, preferred_element_type=F32)


def _dot_nt(a, b):
    return lax.dot_general(a, b, (((1,), (1,)), ((), ())), preferred_element_type=F32)


def _dot_tn(a, b):
    return lax.dot_general(a, b, (((0,), (0,)), ((), ())), preferred_element_type=F32)


def _split(a, n):
    pieces = []
    r = a
    for _ in range(n):
        p = r.astype(BF16)
        pieces.append(p)
        r = r - p.astype(F32)
    return pieces


def _dot_split_lhs(a, m_bf16, n):
    out = None
    for p in _split(a, n):
        t = _dot(p, m_bf16)
        out = t if out is None else out + t
    return out


def _dot_split_rhs(m_bf16, a, n):
    out = None
    for p in _split(a, n):
        t = _dot(m_bf16, p)
        out = t if out is None else out + t
    return out


def _sigmoid(x):
    return 1.0 / (1.0 + jnp.exp(-x))


def _silu(x):
    return x * _sigmoid(x)


def _softplus(x):
    return jnp.maximum(x, 0.0) + jnp.log(1.0 + jnp.exp(-jnp.abs(x)))


def _log_sigmoid(x):
    return jnp.minimum(x, 0.0) - jnp.log(1.0 + jnp.exp(-jnp.abs(x)))


def _cat_blocks(ref, lo, hi):
    return jnp.concatenate([ref[c] for c in range(lo, hi)], axis=-1)


def _inproj0_kernel(x_ref, g_ref, w_ref, wvt_ref, wsh_ref, wsl_ref,
                    proj_ref, vt_ref, sm_ref, smt_ref, xn_scr, *, ncb):
    j = pl.program_id(1)

    @pl.when(j == 0)
    def _():
        x = x_ref[...]
        ms = jnp.mean(x * x, axis=-1, keepdims=True)
        xn = x * lax.rsqrt(ms + EPS) * g_ref[...]
        hi = xn.astype(BF16)
        lo = (xn - hi.astype(F32)).astype(BF16)
        xn_scr[...] = hi
        small = _dot(hi, wsh_ref[...]) + _dot(lo, wsh_ref[...]) + _dot(hi, wsl_ref[...])
        sm_ref[...] = small
        smt_ref[...] = small.T
        vt = _dot_nt(wvt_ref[...], hi)
        for c in range(CB):
            vt_ref[c] = vt[c * LANES:(c + 1) * LANES, :].astype(BF16)

    acc = _dot(xn_scr[...], w_ref[...])
    for c in range(ncb):
        proj_ref[c] = acc[:, c * LANES:(c + 1) * LANES].astype(BF16)


def _inproj0(x2, g, w_main, wvt, wsh, wsl, *, tm, tn):
    m = x2.shape[0]
    ncol = w_main.shape[1] // tn
    ncb = tn // LANES
    kern = functools.partial(_inproj0_kernel, ncb=ncb)
    return pl.pallas_call(
        kern,
        grid=(m // tm, ncol),
        in_specs=[
            pl.BlockSpec((tm, D_MODEL), lambda i, j: (i, 0)),
            pl.BlockSpec((1, D_MODEL), lambda i, j: (0, 0)),
            pl.BlockSpec((D_MODEL, tn), lambda i, j: (0, j)),
            pl.BlockSpec((D_MODEL, D_MODEL), lambda i, j: (0, 0)),
            pl.BlockSpec((D_MODEL, LANES), lambda i, j: (0, 0)),
            pl.BlockSpec((D_MODEL, LANES), lambda i, j: (0, 0)),
        ],
        out_specs=[
            pl.BlockSpec((ncb, tm, LANES), lambda i, j: (j, i, 0)),
            pl.BlockSpec((CB, LANES, tm), lambda i, j: (0, 0, i)),
            pl.BlockSpec((tm, LANES), lambda i, j: (i, 0)),
            pl.BlockSpec((LANES, tm), lambda i, j: (0, i)),
        ],
        out_shape=[
            jax.ShapeDtypeStruct((w_main.shape[1] // LANES, m, LANES), BF16),
            jax.ShapeDtypeStruct((CB, LANES, m), BF16),
            jax.ShapeDtypeStruct((m, LANES), F32),
            jax.ShapeDtypeStruct((LANES, m), F32),
        ],
        scratch_shapes=[pltpu.VMEM((tm, D_MODEL), BF16)],
        compiler_params=pltpu.CompilerParams(
            dimension_semantics=("arbitrary", "arbitrary"), vmem_limit_bytes=VMEM_LIMIT),
        name="inproj0",
    )(x2, g, w_main, wvt, wsh, wsl)


def _ssd_kernel(xbc_ref, z_ref, sm_ref, smt_ref, convw_ref, convb_ref, prow_ref, pcol_ref,
                dskip_ref, norm_ref, e64_ref,
                y_ref, c_ref,
                h_scr, ext_scr, xc_scr, ccarry_scr, sh_scr, *, T):
    t = pl.program_id(1)
    nxc = 2 * CB

    @pl.when(t == 0)
    def _():
        h_scr[...] = jnp.zeros_like(h_scr)
        ext_scr[0:8, :] = jnp.zeros((8, nxc * LANES), F32)
        ccarry_scr[...] = jnp.zeros_like(ccarry_scr)

    for c in range(nxc):
        ext_scr[8:8 + T, c * LANES:(c + 1) * LANES] = xbc_ref[c].astype(F32)
    for c in range(nxc):
        cs_ = slice(c * LANES, (c + 1) * LANES)
        for k in range(SSD_CONV - 1):
            off = SUBLANES - (SSD_CONV - 1) + k
            sh_scr[k] = ext_scr[off:off + T, cs_]
        acc = convb_ref[:, cs_] + (ext_scr[SUBLANES:SUBLANES + T, cs_]
                                   * convw_ref[SSD_CONV - 1:SSD_CONV, cs_])
        for k in range(SSD_CONV - 1):
            acc = acc + sh_scr[k] * convw_ref[k:k + 1, cs_]
        xc_scr[:, cs_] = _silu(acc)
    ext_scr[0:8, :] = ext_scr[T:T + 8, :]

    dtb_row = prow_ref[0:1, :]
    a_row = -jnp.exp(prow_ref[1:2, :])
    fb_row = prow_ref[2:3, :]
    dtb_col = pcol_ref[0]
    a_col = -jnp.exp(pcol_ref[1])

    rt = lax.broadcasted_iota(jnp.int32, (T, T), 0)
    ct = lax.broadcasted_iota(jnp.int32, (T, T), 1)
    ltri_t = jnp.where(rt >= ct, 1.0, 0.0).astype(BF16)
    logf = _log_sigmoid(sm_ref[...] + fb_row)
    ctile = _dot_split_rhs(ltri_t, logf, 3) + ccarry_scr[...]
    c_ref[...] = ctile
    ccarry_scr[...] = ctile[T - 1:T, :]

    ri = lax.broadcasted_iota(jnp.int32, (CHUNK, CHUNK), 0)
    ci = lax.broadcasted_iota(jnp.int32, (CHUNK, CHUNK), 1)
    causal = ri >= ci
    ltri = jnp.where(causal, 1.0, 0.0).astype(BF16)
    utri = jnp.where(ri <= ci, 1.0, 0.0).astype(BF16)
    lane = lax.broadcasted_iota(jnp.int32, (CHUNK, LANES), 1)
    first_head = lane < HEADDIM
    e64 = e64_ref[...]

    for ck in range(T // CHUNK):
        r = slice(ck * CHUNK, (ck + 1) * CHUNK)
        dt = _softplus(sm_ref[r, :] + dtb_row)
        cs = _dot_split_rhs(ltri, dt * a_row, 3)
        dtt = _softplus(smt_ref[0:SSD_HEADS, r] + dtb_col)
        cst = _dot_split_lhs(dtt * a_col, utri, 3)
        cs_e = _dot_split_lhs(cs, e64, 3)
        dt_e = _dot_split_lhs(dt, e64, 3)
        ecs_e = jnp.exp(cs_e)
        dte_e = jnp.exp(cs_e[CHUNK - 1:CHUNK, :] - cs_e)

        for g in range(SSD_GROUPS):
            bsl = slice(D_MODEL + g * D_STATE, D_MODEL + (g + 1) * D_STATE)
            csl = slice(D_MODEL + (SSD_GROUPS + g) * D_STATE,
                        D_MODEL + (SSD_GROUPS + g + 1) * D_STATE)
            bg = xc_scr[r, bsl].astype(BF16)
            cg = xc_scr[r, csl].astype(BF16)
            cb_ = _dot_nt(cg, bg)
            ys = []
            for pp in range(2):
                p = 2 * g + pp
                ps = slice(p * LANES, (p + 1) * LANES)
                x = xc_scr[r, ps]
                xd = x * dt_e[:, ps]
                xd_b = xd.astype(BF16)
                yd = []
                for hh in range(2):
                    h = 2 * p + hh
                    colb = jnp.broadcast_to(cs[:, h:h + 1], (CHUNK, CHUNK))
                    rowb = jnp.broadcast_to(cst[h:h + 1, :], (CHUNK, CHUNK))
                    lm = jnp.where(causal, jnp.exp(colb - rowb), 0.0)
                    yd.append(_dot((cb_ * lm).astype(BF16), xd_b))
                ydiag = jnp.where(first_head, yd[0], yd[1])
                hin = h_scr[:, ps]
                yoff = _dot(cg, hin.astype(BF16)) * ecs_e[:, ps]
                snew = _dot_tn(bg, (xd * dte_e[:, ps]).astype(BF16))
                h_scr[:, ps] = hin * ecs_e[CHUNK - 1:CHUNK, ps] + snew
                yp = ydiag + yoff + x * dskip_ref[:, ps]
                yp = yp * _silu(z_ref[p, r, :].astype(F32))
                ys.append(yp)
            ss = (jnp.sum(ys[0] * ys[0], axis=-1, keepdims=True)
                  + jnp.sum(ys[1] * ys[1], axis=-1, keepdims=True))
            scale = lax.rsqrt(ss * (1.0 / (2 * LANES)) + EPS)
            for pp in range(2):
                p = 2 * g + pp
                ps = slice(p * LANES, (p + 1) * LANES)
                y_ref[p, r, :] = (ys[pp] * scale * norm_ref[:, ps]).astype(BF16)


def _ssd(proj, sm, smt, convw, convb, prow, pcol, dskip_e, norm, e64, *, bsz, seq, T):
    m = bsz * seq
    nt = seq // T
    kern = functools.partial(_ssd_kernel, T=T)
    const2 = lambda b, t: (0, 0)
    return pl.pallas_call(
        kern,
        grid=(bsz, nt),
        in_specs=[
            pl.BlockSpec((2 * CB, T, LANES), lambda b, t: (1, b * nt + t, 0)),
            pl.BlockSpec((CB, T, LANES), lambda b, t: (0, b * nt + t, 0)),
            pl.BlockSpec((T, LANES), lambda b, t: (b * nt + t, 0)),
            pl.BlockSpec((LANES, T), lambda b, t: (0, b * nt + t)),
            pl.BlockSpec(convw.shape, const2),
            pl.BlockSpec(convb.shape, const2),
            pl.BlockSpec(prow.shape, const2),
            pl.BlockSpec(pcol.shape, lambda b, t: (0, 0, 0)),
            pl.BlockSpec(dskip_e.shape, const2),
            pl.BlockSpec(norm.shape, const2),
            pl.BlockSpec(e64.shape, const2),
        ],
        out_specs=[
            pl.BlockSpec((CB, T, LANES), lambda b, t: (0, b * nt + t, 0)),
            pl.BlockSpec((T, LANES), lambda b, t: (b * nt + t, 0)),
        ],
        out_shape=[
            jax.ShapeDtypeStruct((CB, m, LANES), BF16),
            jax.ShapeDtypeStruct((m, LANES), F32),
        ],
        scratch_shapes=[
            pltpu.VMEM((D_STATE, D_MODEL), F32),
            pltpu.VMEM((T + 8, 2 * D_MODEL), F32),
            pltpu.VMEM((T, 2 * D_MODEL), F32),
            pltpu.VMEM((1, LANES), F32),
            pltpu.VMEM((SSD_CONV - 1, T, LANES), F32),
        ],
        compiler_params=pltpu.CompilerParams(
            dimension_semantics=("arbitrary", "arbitrary"), vmem_limit_bytes=VMEM_LIMIT),
        name="ssd",
    )(proj, proj, sm, smt, convw, convb, prow, pcol, dskip_e, norm, e64)


def _fox_kernel(q_ref, k_ref, vt_ref, cq_ref, c_ref, o_ref, qh_scr, m_scr, acc_scr, *,
                tq, tk, c_lane0):
    pair = pl.program_id(1)
    i = pl.program_id(2)
    nstrip = tq // STRIP
    lane_k = lax.broadcasted_iota(jnp.int32, (tk, LANES), 1)
    lane = lax.broadcasted_iota(jnp.int32, (tq, LANES), 1)
    q = q_ref[...]
    zero = jnp.zeros_like(q)
    qh_scr[0] = jnp.where(lane < HEADDIM, q, zero)
    qh_scr[1] = jnp.where(lane >= HEADDIM, q, zero)
    m_scr[...] = jnp.full_like(m_scr, NEG_BIG)
    acc_scr[...] = jnp.zeros_like(acc_scr)
    ones = jnp.ones((ONES_ROWS, tk), BF16)

    def scores(hh, st, nkv, k):
        return _dot_nt(k[0:nkv], qh_scr[hh, st * STRIP:(st + 1) * STRIP, :])

    def softmax_pv(hh, st, nkv, s, ck2, vta, koff, partial):
        cols = slice(st * STRIP, (st + 1) * STRIP)
        t = s - ck2[0:nkv]
        vta = vta[:, 0:nkv]
        if partial:
            kv_pos = koff + lax.broadcasted_iota(jnp.int32, (nkv, STRIP), 0)
            q_pos = i * tq + st * STRIP + lax.broadcasted_iota(jnp.int32, (nkv, STRIP), 1)
            t = jnp.where(kv_pos <= q_pos, t, NEG_BIG)
        cq2 = cq_ref[hh:hh + 1, cols] * LOG2E
        m_prev = m_scr[hh, :, cols]
        m_new = jnp.maximum(m_prev, jnp.max(t, axis=0, keepdims=True) + cq2)
        alpha = jnp.exp2(m_prev - m_new)
        p = jnp.exp2((t - (m_new - cq2)).astype(BF16))
        acc_scr[hh, :, cols] = acc_scr[hh, :, cols] * alpha + _dot(vta, p)
        m_scr[hh, :, cols] = m_new

    def strips_of(diag):
        items = []
        for hh in range(2):
            for st in range(nstrip):
                nkv, partial = tk, False
                if diag is not None:
                    kv_lo, q_lo = diag * tk, st * STRIP
                    if kv_lo > q_lo + STRIP - 1:
                        continue
                    nkv = min(tk, q_lo + STRIP - kv_lo)
                    partial = kv_lo + nkv - 1 > q_lo
                items.append((hh, st, nkv, partial))
        return items

    def blocks(j0, diags):
        koffs, ks, ck2s, vtas, items = [], [], [], [], []
        for b, diag in enumerate(diags):
            koff = pl.multiple_of((j0 + b) * tk, tk)
            cblk = c_ref[pl.ds(koff, tk), :]
            ck2, vta = [], []
            for hh in range(2):
                sel = lane_k == (c_lane0 + 2 * pair + hh)
                ck2.append(jnp.sum(jnp.where(sel, cblk, 0.0), axis=1, keepdims=True) * LOG2E)
                vta.append(jnp.concatenate(
                    [vt_ref[hh * HEADDIM:(hh + 1) * HEADDIM, pl.ds(koff, tk)], ones], axis=0))
            koffs.append(koff)
            ks.append(k_ref[pl.ds(koff, tk), :])
            ck2s.append(ck2)
            vtas.append(vta)
            items += [(b,) + it for it in strips_of(diag)]
        pending = [scores(hh, st, nkv, ks[b]) for b, hh, st, nkv, _ in items[:LOOKAHEAD]]
        for n, (b, hh, st, nkv, partial) in enumerate(items):
            if n + LOOKAHEAD < len(items):
                nb, nhh, nst, nnkv, _ = items[n + LOOKAHEAD]
                pending.append(scores(nhh, nst, nnkv, ks[nb]))
            softmax_pv(hh, st, nkv, pending.pop(0), ck2s[b][hh], vtas[b][hh], koffs[b], partial)

    ndiag = tq // tk
    nfull = i * ndiag
    unroll = math.gcd(ndiag, FOX_UNROLL)

    def body(j, carry):
        blocks(j * unroll, [None] * unroll)
        return carry

    lax.fori_loop(0, nfull // unroll, body, 0)
    blocks(nfull, list(range(ndiag)))

    outs = []
    for hh in range(2):
        acc = acc_scr[hh]
        outs.append(acc[0:HEADDIM, :] * (1.0 / acc[HEADDIM:HEADDIM + 1, :]))
    o_ref[...] = jnp.concatenate(outs, axis=0).T.astype(BF16)


def _fox(proj, vt, c8, c, *, bsz, seq, tq, tk, q_cb0, k_cb0, c_lane0):
    m = bsz * seq
    nq = seq // tq
    npair = FOX_HEADS // 2
    kern = functools.partial(_fox_kernel, tq=tq, tk=tk, c_lane0=c_lane0)
    return pl.pallas_call(
        kern,
        grid=(bsz, npair, nq),
        in_specs=[
            pl.BlockSpec((None, tq, LANES), lambda b, p, i: (q_cb0 + p, b * nq + i, 0)),
            pl.BlockSpec((None, seq, LANES), lambda b, p, i: (k_cb0 + p, b, 0)),
            pl.BlockSpec((None, LANES, seq), lambda b, p, i: (p, 0, b)),
            pl.BlockSpec((None, 8, tq), lambda b, p, i: (p, 0, b * nq + i)),
            pl.BlockSpec((seq, LANES), lambda b, p, i: (b, 0)),
        ],
        out_specs=pl.BlockSpec((None, tq, LANES), lambda b, p, i: (p, b * nq + i, 0)),
        out_shape=jax.ShapeDtypeStruct((npair, m, LANES), BF16),
        scratch_shapes=[
            pltpu.VMEM((2, tq, LANES), BF16),
            pltpu.VMEM((2, 1, tq), F32),
            pltpu.VMEM((2, HEADDIM + ONES_ROWS, tq), F32),
        ],
        compiler_params=pltpu.CompilerParams(
            dimension_semantics=("arbitrary", "arbitrary", "arbitrary"),
            vmem_limit_bytes=VMEM_LIMIT),
        name="fox",
    )(proj, proj, vt, c8, c)


def _outproj0_kernel(y_ref, o_ref, z_ref, x_ref, w_ref, g_ref, out_ref):
    y = _cat_blocks(y_ref, 0, CB)
    o = _cat_blocks(o_ref, 0, CB).astype(F32)
    z = _cat_blocks(z_ref, 0, CB).astype(F32)
    og = (o * _silu(z)).astype(BF16)
    acc = _dot(y, w_ref[0:D_MODEL, :]) + _dot(og, w_ref[D_MODEL:2 * D_MODEL, :])
    ms = jnp.mean(acc * acc, axis=-1, keepdims=True)
    out_ref[...] = x_ref[...] + acc * lax.rsqrt(ms + EPS) * g_ref[...]


def _outproj0(y, o, proj, x2, w, g, *, tm):
    m = x2.shape[0]
    return pl.pallas_call(
        _outproj0_kernel,
        grid=(m // tm,),
        in_specs=[
            pl.BlockSpec((CB, tm, LANES), lambda i: (0, i, 0)),
            pl.BlockSpec((CB, tm, LANES), lambda i: (0, i, 0)),
            pl.BlockSpec((CB, tm, LANES), lambda i: (1, i, 0)),
            pl.BlockSpec((tm, D_MODEL), lambda i: (i, 0)),
            pl.BlockSpec(w.shape, lambda i: (0, 0)),
            pl.BlockSpec((1, D_MODEL), lambda i: (0, 0)),
        ],
        out_specs=pl.BlockSpec((tm, D_MODEL), lambda i: (i, 0)),
        out_shape=jax.ShapeDtypeStruct((m, D_MODEL), F32),
        compiler_params=pltpu.CompilerParams(
            dimension_semantics=("arbitrary",), vmem_limit_bytes=VMEM_LIMIT),
        name="outproj0",
    )(y, o, proj, x2, w, g)


def _inproj1_kernel(x_ref, g_ref, w_ref, h_ref, z_ref, xn_scr, *, seg):
    j = pl.program_id(1)

    @pl.when(j == 0)
    def _():
        x = x_ref[...]
        ms = jnp.mean(x * x, axis=-1, keepdims=True)
        xn_scr[...] = (x * lax.rsqrt(ms + EPS) * g_ref[...]).astype(BF16)

    acc = _dot(xn_scr[...], w_ref[...])
    hv = acc[:, 0:seg] * _sigmoid(acc[:, seg:2 * seg])
    for c in range(seg // LANES):
        h_ref[c] = hv[:, c * LANES:(c + 1) * LANES].astype(BF16)
        z_ref[c] = acc[:, 2 * seg + c * LANES:2 * seg + (c + 1) * LANES].astype(BF16)


def _inproj1(x2, g, w, *, tm, seg):
    m = x2.shape[0]
    tn = 3 * seg
    ncol = w.shape[1] // tn
    scb = seg // LANES
    kern = functools.partial(_inproj1_kernel, seg=seg)
    return pl.pallas_call(
        kern,
        grid=(m // tm, ncol),
        in_specs=[
            pl.BlockSpec((tm, D_MODEL), lambda i, j: (i, 0)),
            pl.BlockSpec((1, D_MODEL), lambda i, j: (0, 0)),
            pl.BlockSpec((D_MODEL, tn), lambda i, j: (0, j)),
        ],
        out_specs=[
            pl.BlockSpec((scb, tm, LANES), lambda i, j: (j, i, 0)),
            pl.BlockSpec((scb, tm, LANES), lambda i, j: (j, i, 0)),
        ],
        out_shape=[
            jax.ShapeDtypeStruct((D_CONV // LANES, m, LANES), BF16),
            jax.ShapeDtypeStruct((D_CONV // LANES, m, LANES), BF16),
        ],
        scratch_shapes=[pltpu.VMEM((tm, D_MODEL), BF16)],
        compiler_params=pltpu.CompilerParams(
            dimension_semantics=("arbitrary", "arbitrary"), vmem_limit_bytes=VMEM_LIMIT),
        name="inproj1",
    )(x2, g, w)


def _conv1_kernel(h_ref, z_ref, x_ref, cw_ref, cb_ref, lg_ref, lb_ref, w_ref, g_ref,
                  out_ref, ext_scr, hc_scr, sh_scr, *, T):
    t = pl.program_id(1)
    ncb = D_CONV // LANES
    halo = CONV_HALO

    @pl.when(t == 0)
    def _():
        ext_scr[0:halo, :] = jnp.zeros((halo, D_CONV), F32)

    for c in range(ncb):
        ext_scr[halo:halo + T, c * LANES:(c + 1) * LANES] = h_ref[c].astype(F32)
    span = T + halo - SUBLANES
    for c in range(ncb):
        cs_ = slice(c * LANES, (c + 1) * LANES)
        for r in range(1, SUBLANES):
            sh_scr[r - 1] = ext_scr[r:r + span, cs_]
        acc = cb_ref[:, cs_]
        for k in range(CONV_WIDTH):
            base, r = divmod(halo - (CONV_WIDTH - 1) + k, SUBLANES)
            base *= SUBLANES
            if r == 0:
                src = ext_scr[base:base + T, cs_]
            else:
                src = sh_scr[r - 1, base:base + T, :]
            acc = acc + src * cw_ref[k:k + 1, cs_]
        hc_scr[:, cs_] = acc
    ext_scr[0:halo, :] = ext_scr[T:T + halo, :]

    hc = hc_scr[...]
    mu = jnp.mean(hc, axis=-1, keepdims=True)
    xc = hc - mu
    var = jnp.mean(xc * xc, axis=-1, keepdims=True)
    hn = xc * lax.rsqrt(var + EPS) * lg_ref[...] + lb_ref[...]
    z = _cat_blocks(z_ref, 0, ncb).astype(F32)
    hg = (_silu(hn) * _silu(z)).astype(BF16)
    acc = _dot(hg, w_ref[...])
    ms = jnp.mean(acc * acc, axis=-1, keepdims=True)
    out_ref[...] = x_ref[...] + acc * lax.rsqrt(ms + EPS) * g_ref[...]


def _conv1(h, z, x2, cw, cb, lg, lb, w, g, *, bsz, seq, T):
    m = bsz * seq
    nt = seq // T
    ncb = D_CONV // LANES
    kern = functools.partial(_conv1_kernel, T=T)
    const2 = lambda b, t: (0, 0)
    return pl.pallas_call(
        kern,
        grid=(bsz, nt),
        in_specs=[
            pl.BlockSpec((ncb, T, LANES), lambda b, t: (0, b * nt + t, 0)),
            pl.BlockSpec((ncb, T, LANES), lambda b, t: (0, b * nt + t, 0)),
            pl.BlockSpec((T, D_MODEL), lambda b, t: (b * nt + t, 0)),
            pl.BlockSpec(cw.shape, const2),
            pl.BlockSpec(cb.shape, const2),
            pl.BlockSpec(lg.shape, const2),
            pl.BlockSpec(lb.shape, const2),
            pl.BlockSpec(w.shape, const2),
            pl.BlockSpec(g.shape, const2),
        ],
        out_specs=pl.BlockSpec((T, D_MODEL), lambda b, t: (b * nt + t, 0)),
        out_shape=jax.ShapeDtypeStruct((m, D_MODEL), F32),
        scratch_shapes=[
            pltpu.VMEM((T + CONV_HALO, D_CONV), F32),
            pltpu.VMEM((T, D_CONV), F32),
            pltpu.VMEM((SUBLANES - 1, T + CONV_HALO - SUBLANES, LANES), F32),
        ],
        compiler_params=pltpu.CompilerParams(
            dimension_semantics=("arbitrary", "arbitrary"), vmem_limit_bytes=VMEM_LIMIT),
        name="conv1",
    )(h, z, x2, cw, cb, lg, lb, w, g)


def _pad_lanes(v, width=LANES):
    return jnp.pad(v, (0, width - v.shape[0]))


def _even_layer(x2, bsz, seq, g_pre, w_in, conv_w, conv_b, dt_bias, a_log, d_skip, fgate_b,
                ssd_norm, w_out, g_post):
    d_xbc = D_MODEL + 2 * SSD_GROUPS * D_STATE
    o_z, o_xbc = 0, 2 * D_MODEL
    o_dt = o_xbc + d_xbc
    o_q = o_dt + SSD_HEADS
    o_k = o_q + D_MODEL
    o_v = o_k + D_MODEL
    o_f = o_v + D_MODEL
    q_scale = HEADDIM ** -0.5 * LOG2E
    w_main = jnp.concatenate(
        [w_in[:, o_z:o_dt], w_in[:, o_q:o_k] * q_scale, w_in[:, o_k:o_v]],
        axis=1).astype(BF16)
    wvt = w_in[:, o_v:o_f].T.astype(BF16)
    w_small = jnp.concatenate(
        [w_in[:, o_dt:o_q], w_in[:, o_f:o_f + FOX_HEADS],
         jnp.zeros((D_MODEL, LANES - SSD_HEADS - FOX_HEADS), F32)], axis=1)
    wsh = w_small.astype(BF16)
    wsl = (w_small - wsh.astype(F32)).astype(BF16)

    proj, vt, sm, smt = _inproj0(x2, g_pre[None, :], w_main, wvt, wsh, wsl,
                                 tm=TILES.proj_rows, tn=TILES.proj_cols)

    zeros16 = jnp.zeros((SSD_HEADS,), F32)
    prow = jnp.stack([
        _pad_lanes(dt_bias),
        _pad_lanes(a_log),
        _pad_lanes(jnp.concatenate([zeros16, fgate_b])),
    ] + [jnp.zeros((LANES,), F32)] * 5)
    pcol = jnp.stack([jnp.broadcast_to(dt_bias[:, None], (SSD_HEADS, LANES)),
                      jnp.broadcast_to(a_log[:, None], (SSD_HEADS, LANES))])
    dskip_e = jnp.repeat(d_skip, HEADDIM)[None, :]
    head_of_lane = jnp.arange(D_MODEL) // HEADDIM
    e64 = (jnp.arange(LANES)[:, None] == head_of_lane[None, :]).astype(BF16)

    y, c = _ssd(proj, sm, smt, conv_w, conv_b[None, :], prow, pcol, dskip_e,
                ssd_norm[None, :], e64, bsz=bsz, seq=seq, T=TILES.ssd_rows)

    ct = c[:, SSD_HEADS:SSD_HEADS + FOX_HEADS].T.reshape(FOX_HEADS // 2, 2, -1)
    c8 = jnp.pad(ct, ((0, 0), (0, 6), (0, 0)))

    o = _fox(proj, vt, c8, c, bsz=bsz, seq=seq, tq=TILES.fox_q, tk=TILES.fox_kv,
             q_cb0=4 * CB, k_cb0=5 * CB, c_lane0=SSD_HEADS)
    return _outproj0(y, o, proj, x2, w_out.astype(BF16), g_post[None, :], tm=TILES.out_rows)


def _odd_layer(x2, bsz, seq, g_pre, w_in, conv_w, conv_b, ln_g, ln_b, w_out, g_post):
    seg = TILES.glu_seg
    cols = []
    for j in range(D_CONV // seg):
        for part in range(3):
            cols.append(w_in[:, part * D_CONV + j * seg:part * D_CONV + (j + 1) * seg])
    w = jnp.concatenate(cols, axis=1).astype(BF16)
    h, z = _inproj1(x2, g_pre[None, :], w, tm=TILES.proj_rows, seg=seg)
    return _conv1(h, z, x2, conv_w, conv_b[None, :], ln_g[None, :], ln_b[None, :],
                  w_out.astype(BF16), g_post[None, :], bsz=bsz, seq=seq, T=TILES.conv_rows)


def kernel(x, e_norm_pre, e_w_in, e_conv_w, e_conv_b, e_dt_bias, e_a_log, e_d_skip, e_fgate_b,
           e_ssd_norm, e_w_out, e_norm_post, o_norm_pre, o_w_in, o_conv_w, o_conv_b, o_ln_g,
           o_ln_b, o_w_out, o_norm_post):
    bsz, seq, d = x.shape
    x2 = x.reshape(bsz * seq, d)
    depth = e_w_in.shape[0] + o_w_in.shape[0]
    for layer in range(depth):
        i = layer // 2
        if layer % 2 == 0:
            x2 = _even_layer(x2, bsz, seq, e_norm_pre[i], e_w_in[i], e_conv_w[i], e_conv_b[i],
                             e_dt_bias[i], e_a_log[i], e_d_skip[i], e_fgate_b[i],
                             e_ssd_norm[i], e_w_out[i], e_norm_post[i])
        else:
            x2 = _odd_layer(x2, bsz, seq, o_norm_pre[i], o_w_in[i], o_conv_w[i], o_conv_b[i],
                            o_ln_g[i], o_ln_b[i], o_w_out[i], o_norm_post[i])
    return x2.reshape(bsz, seq, d)
```

```python
import functools
import math
from typing import NamedTuple

import jax
import jax.numpy as jnp
from jax import lax
from jax.experimental import pallas as pl
from jax.experimental.pallas import tpu as pltpu

F32 = jnp.float32
BF16 = jnp.bfloat16

LANES = 128
SUBLANES = 8
D_MODEL = 1024
SSD_HEADS = 16
SSD_GROUPS = 4
HEADDIM = 64
D_STATE = 128
CHUNK = 128
SSD_CONV = 4
FOX_HEADS = 16
D_CONV = 2 * D_MODEL
CONV_WIDTH = 31
EPS = 1e-6
NEG_BIG = -1e30
LOG2E = 1.4426950408889634
STRIP = 256
ONES_ROWS = 16
CONV_HALO = 32
LOOKAHEAD = 4
FOX_UNROLL = 4
VMEM_LIMIT = 56 * 1024 * 1024

CB = D_MODEL // LANES


class _Tiles(NamedTuple):
    proj_rows: int = 1024
    proj_cols: int = 2048
    glu_seg: int = 1024
    ssd_rows: int = 256
    fox_q: int = 2048
    fox_kv: int = 512
    out_rows: int = 1024
    conv_rows: int = 512


TILES = _Tiles()


def _dot(a, b):
    return jnp.dot(a, b, preferred_element_type=F32)


def _dot_nt(a, b):
    return lax.dot_general(a, b, (((1,), (1,)), ((), ())), preferred_element_type=F32)


def _dot_tn(a, b):
    return lax.dot_general(a, b, (((0,), (0,)), ((), ())), preferred_element_type=F32)


def _split(a, n):
    pieces = []
    r = a
    for _ in range(n):
        p = r.astype(BF16)
        pieces.append(p)
        r = r - p.astype(F32)
    return pieces


def _dot_split_lhs(a, m_bf16, n):
    out = None
    for p in _split(a, n):
        t = _dot(p, m_bf16)
        out = t if out is None else out + t
    return out


def _dot_split_rhs(m_bf16, a, n):
    out = None
    for p in _split(a, n):
        t = _dot(m_bf16, p)
        out = t if out is None else out + t
    return out


def _sigmoid(x):
    return 1.0 / (1.0 + jnp.exp(-x))


def _silu(x):
    return x * _sigmoid(x)


def _softplus(x):
    return jnp.maximum(x, 0.0) + jnp.log(1.0 + jnp.exp(-jnp.abs(x)))


def _log_sigmoid(x):
    return jnp.minimum(x, 0.0) - jnp.log(1.0 + jnp.exp(-jnp.abs(x)))


def _cat_blocks(ref, lo, hi):
    return jnp.concatenate([ref[c] for c in range(lo, hi)], axis=-1)


def _inproj0_kernel(x_ref, g_ref, w_ref, wvt_ref, wsh_ref, wsl_ref,
                    proj_ref, vt_ref, sm_ref, smt_ref, xn_scr, *, ncb):
    j = pl.program_id(1)

    @pl.when(j == 0)
    def _():
        x = x_ref[...]
        ms = jnp.mean(x * x, axis=-1, keepdims=True)
        xn = x * lax.rsqrt(ms + EPS) * g_ref[...]
        hi = xn.astype(BF16)
        lo = (xn - hi.astype(F32)).astype(BF16)
        xn_scr[...] = hi
        small = _dot(hi, wsh_ref[...]) + _dot(lo, wsh_ref[...]) + _dot(hi, wsl_ref[...])
        sm_ref[...] = small
        smt_ref[...] = small.T
        vt = _dot_nt(wvt_ref[...], hi)
        for c in range(CB):
            vt_ref[c] = vt[c * LANES:(c + 1) * LANES, :].astype(BF16)

    acc = _dot(xn_scr[...], w_ref[...])
    for c in range(ncb):
        proj_ref[c] = acc[:, c * LANES:(c + 1) * LANES].astype(BF16)


def _inproj0(x2, g, w_main, wvt, wsh, wsl, *, tm, tn):
    m = x2.shape[0]
    ncol = w_main.shape[1] // tn
    ncb = tn // LANES
    kern = functools.partial(_inproj0_kernel, ncb=ncb)
    return pl.pallas_call(
        kern,
        grid=(m // tm, ncol),
        in_specs=[
            pl.BlockSpec((tm, D_MODEL), lambda i, j: (i, 0)),
            pl.BlockSpec((1, D_MODEL), lambda i, j: (0, 0)),
            pl.BlockSpec((D_MODEL, tn), lambda i, j: (0, j)),
            pl.BlockSpec((D_MODEL, D_MODEL), lambda i, j: (0, 0)),
            pl.BlockSpec((D_MODEL, LANES), lambda i, j: (0, 0)),
            pl.BlockSpec((D_MODEL, LANES), lambda i, j: (0, 0)),
        ],
        out_specs=[
            pl.BlockSpec((ncb, tm, LANES), lambda i, j: (j, i, 0)),
            pl.BlockSpec((CB, LANES, tm), lambda i, j: (0, 0, i)),
            pl.BlockSpec((tm, LANES), lambda i, j: (i, 0)),
            pl.BlockSpec((LANES, tm), lambda i, j: (0, i)),
        ],
        out_shape=[
            jax.ShapeDtypeStruct((w_main.shape[1] // LANES, m, LANES), BF16),
            jax.ShapeDtypeStruct((CB, LANES, m), BF16),
            jax.ShapeDtypeStruct((m, LANES), F32),
            jax.ShapeDtypeStruct((LANES, m), F32),
        ],
        scratch_shapes=[pltpu.VMEM((tm, D_MODEL), BF16)],
        compiler_params=pltpu.CompilerParams(
            dimension_semantics=("arbitrary", "arbitrary"), vmem_limit_bytes=VMEM_LIMIT),
        name="inproj0",
    )(x2, g, w_main, wvt, wsh, wsl)


def _ssd_kernel(xbc_ref, z_ref, sm_ref, smt_ref, convw_ref, convb_ref, prow_ref, pcol_ref,
                dskip_ref, norm_ref, e64_ref,
                y_ref, c_ref,
                h_scr, ext_scr, xc_scr, ccarry_scr, sh_scr, *, T):
    t = pl.program_id(1)
    nxc = 2 * CB

    @pl.when(t == 0)
    def _():
        h_scr[...] = jnp.zeros_like(h_scr)
        ext_scr[0:8, :] = jnp.zeros((8, nxc * LANES), F32)
        ccarry_scr[...] = jnp.zeros_like(ccarry_scr)

    for c in range(nxc):
        ext_scr[8:8 + T, c * LANES:(c + 1) * LANES] = xbc_ref[c].astype(F32)
    for c in range(nxc):
        cs_ = slice(c * LANES, (c + 1) * LANES)
        for k in range(SSD_CONV - 1):
            off = SUBLANES - (SSD_CONV - 1) + k
            sh_scr[k] = ext_scr[off:off + T, cs_]
        acc = convb_ref[:, cs_] + (ext_scr[SUBLANES:SUBLANES + T, cs_]
                                   * convw_ref[SSD_CONV - 1:SSD_CONV, cs_])
        for k in range(SSD_CONV - 1):
            acc = acc + sh_scr[k] * convw_ref[k:k + 1, cs_]
        xc_scr[:, cs_] = _silu(acc)
    ext_scr[0:8, :] = ext_scr[T:T + 8, :]

    dtb_row = prow_ref[0:1, :]
    a_row = -jnp.exp(prow_ref[1:2, :])
    fb_row = prow_ref[2:3, :]
    dtb_col = pcol_ref[0]
    a_col = -jnp.exp(pcol_ref[1])

    rt = lax.broadcasted_iota(jnp.int32, (T, T), 0)
    ct = lax.broadcasted_iota(jnp.int32, (T, T), 1)
    ltri_t = jnp.where(rt >= ct, 1.0, 0.0).astype(BF16)
    logf = _log_sigmoid(sm_ref[...] + fb_row)
    ctile = _dot_split_rhs(ltri_t, logf, 3) + ccarry_scr[...]
    c_ref[...] = ctile
    ccarry_scr[...] = ctile[T - 1:T, :]

    ri = lax.broadcasted_iota(jnp.int32, (CHUNK, CHUNK), 0)
    ci = lax.broadcasted_iota(jnp.int32, (CHUNK, CHUNK), 1)
    causal = ri >= ci
    ltri = jnp.where(causal, 1.0, 0.0).astype(BF16)
    utri = jnp.where(ri <= ci, 1.0, 0.0).astype(BF16)
    lane = lax.broadcasted_iota(jnp.int32, (CHUNK, LANES), 1)
    first_head = lane < HEADDIM
    e64 = e64_ref[...]

    for ck in range(T // CHUNK):
        r = slice(ck * CHUNK, (ck + 1) * CHUNK)
        dt = _softplus(sm_ref[r, :] + dtb_row)
        cs = _dot_split_rhs(ltri, dt * a_row, 3)
        dtt = _softplus(smt_ref[0:SSD_HEADS, r] + dtb_col)
        cst = _dot_split_lhs(dtt * a_col, utri, 3)
        cs_e = _dot_split_lhs(cs, e64, 3)
        dt_e = _dot_split_lhs(dt, e64, 3)
        ecs_e = jnp.exp(cs_e)
        dte_e = jnp.exp(cs_e[CHUNK - 1:CHUNK, :] - cs_e)

        for g in range(SSD_GROUPS):
            bsl = slice(D_MODEL + g * D_STATE, D_MODEL + (g + 1) * D_STATE)
            csl = slice(D_MODEL + (SSD_GROUPS + g) * D_STATE,
                        D_MODEL + (SSD_GROUPS + g + 1) * D_STATE)
            bg = xc_scr[r, bsl].astype(BF16)
            cg = xc_scr[r, csl].astype(BF16)
            cb_ = _dot_nt(cg, bg)
            ys = []
            for pp in range(2):
                p = 2 * g + pp
                ps = slice(p * LANES, (p + 1) * LANES)
                x = xc_scr[r, ps]
                xd = x * dt_e[:, ps]
                xd_b = xd.astype(BF16)
                yd = []
                for hh in range(2):
                    h = 2 * p + hh
                    colb = jnp.broadcast_to(cs[:, h:h + 1], (CHUNK, CHUNK))
                    rowb = jnp.broadcast_to(cst[h:h + 1, :], (CHUNK, CHUNK))
                    lm = jnp.where(causal, jnp.exp(colb - rowb), 0.0)
                    yd.append(_dot((cb_ * lm).astype(BF16), xd_b))
                ydiag = jnp.where(first_head, yd[0], yd[1])
                hin = h_scr[:, ps]
                yoff = _dot(cg, hin.astype(BF16)) * ecs_e[:, ps]
                snew = _dot_tn(bg, (xd * dte_e[:, ps]).astype(BF16))
                h_scr[:, ps] = hin * ecs_e[CHUNK - 1:CHUNK, ps] + snew
                yp = ydiag + yoff + x * dskip_ref[:, ps]
                yp = yp * _silu(z_ref[p, r, :].astype(F32))
                ys.append(yp)
            ss = (jnp.sum(ys[0] * ys[0], axis=-1, keepdims=True)
                  + jnp.sum(ys[1] * ys[1], axis=-1, keepdims=True))
            scale = lax.rsqrt(ss * (1.0 / (2 * LANES)) + EPS)
            for pp in range(2):
                p = 2 * g + pp
                ps = slice(p * LANES, (p + 1) * LANES)
                y_ref[p, r, :] = (ys[pp] * scale * norm_ref[:, ps]).astype(BF16)


def _ssd(proj, sm, smt, convw, convb, prow, pcol, dskip_e, norm, e64, *, bsz, seq, T):
    m = bsz * seq
    nt = seq // T
    kern = functools.partial(_ssd_kernel, T=T)
    const2 = lambda b, t: (0, 0)
    return pl.pallas_call(
        kern,
        grid=(bsz, nt),
        in_specs=[
            pl.BlockSpec((2 * CB, T, LANES), lambda b, t: (1, b * nt + t, 0)),
            pl.BlockSpec((CB, T, LANES), lambda b, t: (0, b * nt + t, 0)),
            pl.BlockSpec((T, LANES), lambda b, t: (b * nt + t, 0)),
            pl.BlockSpec((LANES, T), lambda b, t: (0, b * nt + t)),
            pl.BlockSpec(convw.shape, const2),
            pl.BlockSpec(convb.shape, const2),
            pl.BlockSpec(prow.shape, const2),
            pl.BlockSpec(pcol.shape, lambda b, t: (0, 0, 0)),
            pl.BlockSpec(dskip_e.shape, const2),
            pl.BlockSpec(norm.shape, const2),
            pl.BlockSpec(e64.shape, const2),
        ],
        out_specs=[
            pl.BlockSpec((CB, T, LANES), lambda b, t: (0, b * nt + t, 0)),
            pl.BlockSpec((T, LANES), lambda b, t: (b * nt + t, 0)),
        ],
        out_shape=[
            jax.ShapeDtypeStruct((CB, m, LANES), BF16),
            jax.ShapeDtypeStruct((m, LANES), F32),
        ],
        scratch_shapes=[
            pltpu.VMEM((D_STATE, D_MODEL), F32),
            pltpu.VMEM((T + 8, 2 * D_MODEL), F32),
            pltpu.VMEM((T, 2 * D_MODEL), F32),
            pltpu.VMEM((1, LANES), F32),
            pltpu.VMEM((SSD_CONV - 1, T, LANES), F32),
        ],
        compiler_params=pltpu.CompilerParams(
            dimension_semantics=("arbitrary", "arbitrary"), vmem_limit_bytes=VMEM_LIMIT),
        name="ssd",
    )(proj, proj, sm, smt, convw, convb, prow, pcol, dskip_e, norm, e64)


def _fox_kernel(q_ref, k_ref, vt_ref, cq_ref, c_ref, o_ref, qh_scr, m_scr, acc_scr, *,
                tq, tk, c_lane0):
    pair = pl.program_id(1)
    i = pl.program_id(2)
    nstrip = tq // STRIP
    lane_k = lax.broadcasted_iota(jnp.int32, (tk, LANES), 1)
    lane = lax.broadcasted_iota(jnp.int32, (tq, LANES), 1)
    q = q_ref[...]
    zero = jnp.zeros_like(q)
    qh_scr[0] = jnp.where(lane < HEADDIM, q, zero)
    qh_scr[1] = jnp.where(lane >= HEADDIM, q, zero)
    m_scr[...] = jnp.full_like(m_scr, NEG_BIG)
    acc_scr[...] = jnp.zeros_like(acc_scr)
    ones = jnp.ones((ONES_ROWS, tk), BF16)

    def scores(hh, st, nkv, k):
        return _dot_nt(k[0:nkv], qh_scr[hh, st * STRIP:(st + 1) * STRIP, :])

    def softmax_pv(hh, st, nkv, s, ck2, vta, koff, partial):
        cols = slice(st * STRIP, (st + 1) * STRIP)
        t = s - ck2[0:nkv]
        vta = vta[:, 0:nkv]
        if partial:
            kv_pos = koff + lax.broadcasted_iota(jnp.int32, (nkv, STRIP), 0)
            q_pos = i * tq + st * STRIP + lax.broadcasted_iota(jnp.int32, (nkv, STRIP), 1)
            t = jnp.where(kv_pos <= q_pos, t, NEG_BIG)
        cq2 = cq_ref[hh:hh + 1, cols] * LOG2E
        m_prev = m_scr[hh, :, cols]
        m_new = jnp.maximum(m_prev, jnp.max(t, axis=0, keepdims=True) + cq2)
        alpha = jnp.exp2(m_prev - m_new)
        p = jnp.exp2((t - (m_new - cq2)).astype(BF16))
        acc_scr[hh, :, cols] = acc_scr[hh, :, cols] * alpha + _dot(vta, p)
        m_scr[hh, :, cols] = m_new

    def strips_of(diag):
        items = []
        for hh in range(2):
            for st in range(nstrip):
                nkv, partial = tk, False
                if diag is not None:
                    kv_lo, q_lo = diag * tk, st * STRIP
                    if kv_lo > q_lo + STRIP - 1:
                        continue
                    nkv = min(tk, q_lo + STRIP - kv_lo)
                    partial = kv_lo + nkv - 1 > q_lo
                items.append((hh, st, nkv, partial))
        return items

    def blocks(j0, diags):
        koffs, ks, ck2s, vtas, items = [], [], [], [], []
        for b, diag in enumerate(diags):
            koff = pl.multiple_of((j0 + b) * tk, tk)
            cblk = c_ref[pl.ds(koff, tk), :]
            ck2, vta = [], []
            for hh in range(2):
                sel = lane_k == (c_lane0 + 2 * pair + hh)
                ck2.append(jnp.sum(jnp.where(sel, cblk, 0.0), axis=1, keepdims=True) * LOG2E)
                vta.append(jnp.concatenate(
                    [vt_ref[hh * HEADDIM:(hh + 1) * HEADDIM, pl.ds(koff, tk)], ones], axis=0))
            koffs.append(koff)
            ks.append(k_ref[pl.ds(koff, tk), :])
            ck2s.append(ck2)
            vtas.append(vta)
            items += [(b,) + it for it in strips_of(diag)]
        pending = [scores(hh, st, nkv, ks[b]) for b, hh, st, nkv, _ in items[:LOOKAHEAD]]
        for n, (b, hh, st, nkv, partial) in enumerate(items):
            if n + LOOKAHEAD < len(items):
                nb, nhh, nst, nnkv, _ = items[n + LOOKAHEAD]
                pending.append(scores(nhh, nst, nnkv, ks[nb]))
            softmax_pv(hh, st, nkv, pending.pop(0), ck2s[b][hh], vtas[b][hh], koffs[b], partial)

    ndiag = tq // tk
    nfull = i * ndiag
    unroll = math.gcd(ndiag, FOX_UNROLL)

    def body(j, carry):
        blocks(j * unroll, [None] * unroll)
        return carry

    lax.fori_loop(0, nfull // unroll, body, 0)
    blocks(nfull, list(range(ndiag)))

    outs = []
    for hh in range(2):
        acc = acc_scr[hh]
        outs.append(acc[0:HEADDIM, :] * (1.0 / acc[HEADDIM:HEADDIM + 1, :]))
    o_ref[...] = jnp.concatenate(outs, axis=0).T.astype(BF16)


def _fox(proj, vt, c8, c, *, bsz, seq, tq, tk, q_cb0, k_cb0, c_lane0):
    m = bsz * seq
    nq = seq // tq
    npair = FOX_HEADS // 2
    kern = functools.partial(_fox_kernel, tq=tq, tk=tk, c_lane0=c_lane0)
    return pl.pallas_call(
        kern,
        grid=(bsz, npair, nq),
        in_specs=[
            pl.BlockSpec((None, tq, LANES), lambda b, p, i: (q_cb0 + p, b * nq + i, 0)),
            pl.BlockSpec((None, seq, LANES), lambda b, p, i: (k_cb0 + p, b, 0)),
            pl.BlockSpec((None, LANES, seq), lambda b, p, i: (p, 0, b)),
            pl.BlockSpec((None, 8, tq), lambda b, p, i: (p, 0, b * nq + i)),
            pl.BlockSpec((seq, LANES), lambda b, p, i: (b, 0)),
        ],
        out_specs=pl.BlockSpec((None, tq, LANES), lambda b, p, i: (p, b * nq + i, 0)),
        out_shape=jax.ShapeDtypeStruct((npair, m, LANES), BF16),
        scratch_shapes=[
            pltpu.VMEM((2, tq, LANES), BF16),
            pltpu.VMEM((2, 1, tq), F32),
            pltpu.VMEM((2, HEADDIM + ONES_ROWS, tq), F32),
        ],
        compiler_params=pltpu.CompilerParams(
            dimension_semantics=("arbitrary", "arbitrary", "arbitrary"),
            vmem_limit_bytes=VMEM_LIMIT),
        name="fox",
    )(proj, proj, vt, c8, c)


def _outproj0_kernel(y_ref, o_ref, z_ref, x_ref, w_ref, g_ref, out_ref):
    y = _cat_blocks(y_ref, 0, CB)
    o = _cat_blocks(o_ref, 0, CB).astype(F32)
    z = _cat_blocks(z_ref, 0, CB).astype(F32)
    og = (o * _silu(z)).astype(BF16)
    acc = _dot(y, w_ref[0:D_MODEL, :]) + _dot(og, w_ref[D_MODEL:2 * D_MODEL, :])
    ms = jnp.mean(acc * acc, axis=-1, keepdims=True)
    out_ref[...] = x_ref[...] + acc * lax.rsqrt(ms + EPS) * g_ref[...]


def _outproj0(y, o, proj, x2, w, g, *, tm):
    m = x2.shape[0]
    return pl.pallas_call(
        _outproj0_kernel,
        grid=(m // tm,),
        in_specs=[
            pl.BlockSpec((CB, tm, LANES), lambda i: (0, i, 0)),
            pl.BlockSpec((CB, tm, LANES), lambda i: (0, i, 0)),
            pl.BlockSpec((CB, tm, LANES), lambda i: (1, i, 0)),
            pl.BlockSpec((tm, D_MODEL), lambda i: (i, 0)),
            pl.BlockSpec(w.shape, lambda i: (0, 0)),
            pl.BlockSpec((1, D_MODEL), lambda i: (0, 0)),
        ],
        out_specs=pl.BlockSpec((tm, D_MODEL), lambda i: (i, 0)),
        out_shape=jax.ShapeDtypeStruct((m, D_MODEL), F32),
        compiler_params=pltpu.CompilerParams(
            dimension_semantics=("arbitrary",), vmem_limit_bytes=VMEM_LIMIT),
        name="outproj0",
    )(y, o, proj, x2, w, g)


def _inproj1_kernel(x_ref, g_ref, w_ref, h_ref, z_ref, xn_scr, *, seg):
    j = pl.program_id(1)

    @pl.when(j == 0)
    def _():
        x = x_ref[...]
        ms = jnp.mean(x * x, axis=-1, keepdims=True)
        xn_scr[...] = (x * lax.rsqrt(ms + EPS) * g_ref[...]).astype(BF16)

    acc = _dot(xn_scr[...], w_ref[...])
    hv = acc[:, 0:seg] * _sigmoid(acc[:, seg:2 * seg])
    for c in range(seg // LANES):
        h_ref[c] = hv[:, c * LANES:(c + 1) * LANES].astype(BF16)
        z_ref[c] = acc[:, 2 * seg + c * LANES:2 * seg + (c + 1) * LANES].astype(BF16)


def _inproj1(x2, g, w, *, tm, seg):
    m = x2.shape[0]
    tn = 3 * seg
    ncol = w.shape[1] // tn
    scb = seg // LANES
    kern = functools.partial(_inproj1_kernel, seg=seg)
    return pl.pallas_call(
        kern,
        grid=(m // tm, ncol),
        in_specs=[
            pl.BlockSpec((tm, D_MODEL), lambda i, j: (i, 0)),
            pl.BlockSpec((1, D_MODEL), lambda i, j: (0, 0)),
            pl.BlockSpec((D_MODEL, tn), lambda i, j: (0, j)),
        ],
        out_specs=[
            pl.BlockSpec((scb, tm, LANES), lambda i, j: (j, i, 0)),
            pl.BlockSpec((scb, tm, LANES), lambda i, j: (j, i, 0)),
        ],
        out_shape=[
            jax.ShapeDtypeStruct((D_CONV // LANES, m, LANES), BF16),
            jax.ShapeDtypeStruct((D_CONV // LANES, m, LANES), BF16),
        ],
        scratch_shapes=[pltpu.VMEM((tm, D_MODEL), BF16)],
        compiler_params=pltpu.CompilerParams(
            dimension_semantics=("arbitrary", "arbitrary"), vmem_limit_bytes=VMEM_LIMIT),
        name="inproj1",
    )(x2, g, w)


def _conv1_kernel(h_ref, z_ref, x_ref, cw_ref, cb_ref, lg_ref, lb_ref, w_ref, g_ref,
                  out_ref, ext_scr, hc_scr, sh_scr, *, T):
    t = pl.program_id(1)
    ncb = D_CONV // LANES
    halo = CONV_HALO

    @pl.when(t == 0)
    def _():
        ext_scr[0:halo, :] = jnp.zeros((halo, D_CONV), F32)

    for c in range(ncb):
        ext_scr[halo:halo + T, c * LANES:(c + 1) * LANES] = h_ref[c].astype(F32)
    span = T + halo - SUBLANES
    for c in range(ncb):
        cs_ = slice(c * LANES, (c + 1) * LANES)
        for r in range(1, SUBLANES):
            sh_scr[r - 1] = ext_scr[r:r + span, cs_]
        acc = cb_ref[:, cs_]
        for k in range(CONV_WIDTH):
            base, r = divmod(halo - (CONV_WIDTH - 1) + k, SUBLANES)
            base *= SUBLANES
            if r == 0:
                src = ext_scr[base:base + T, cs_]
            else:
                src = sh_scr[r - 1, base:base + T, :]
            acc = acc + src * cw_ref[k:k + 1, cs_]
        hc_scr[:, cs_] = acc
    ext_scr[0:halo, :] = ext_scr[T:T + halo, :]

    hc = hc_scr[...]
    mu = jnp.mean(hc, axis=-1, keepdims=True)
    xc = hc - mu
    var = jnp.mean(xc * xc, axis=-1, keepdims=True)
    hn = xc * lax.rsqrt(var + EPS) * lg_ref[...] + lb_ref[...]
    z = _cat_blocks(z_ref, 0, ncb).astype(F32)
    hg = (_silu(hn) * _silu(z)).astype(BF16)
    acc = _dot(hg, w_ref[...])
    ms = jnp.mean(acc * acc, axis=-1, keepdims=True)
    out_ref[...] = x_ref[...] + acc * lax.rsqrt(ms + EPS) * g_ref[...]


def _conv1(h, z, x2, cw, cb, lg, lb, w, g, *, bsz, seq, T):
    m = bsz * seq
    nt = seq // T
    ncb = D_CONV // LANES
    kern = functools.partial(_conv1_kernel, T=T)
    const2 = lambda b, t: (0, 0)
    return pl.pallas_call(
        kern,
        grid=(bsz, nt),
        in_specs=[
            pl.BlockSpec((ncb, T, LANES), lambda b, t: (0, b * nt + t, 0)),
            pl.BlockSpec((ncb, T, LANES), lambda b, t: (0, b * nt + t, 0)),
            pl.BlockSpec((T, D_MODEL), lambda b, t: (b * nt + t, 0)),
            pl.BlockSpec(cw.shape, const2),
            pl.BlockSpec(cb.shape, const2),
            pl.BlockSpec(lg.shape, const2),
            pl.BlockSpec(lb.shape, const2),
            pl.BlockSpec(w.shape, const2),
            pl.BlockSpec(g.shape, const2),
        ],
        out_specs=pl.BlockSpec((T, D_MODEL), lambda b, t: (b * nt + t, 0)),
        out_shape=jax.ShapeDtypeStruct((m, D_MODEL), F32),
        scratch_shapes=[
            pltpu.VMEM((T + CONV_HALO, D_CONV), F32),
            pltpu.VMEM((T, D_CONV), F32),
            pltpu.VMEM((SUBLANES - 1, T + CONV_HALO - SUBLANES, LANES), F32),
        ],
        compiler_params=pltpu.CompilerParams(
            dimension_semantics=("arbitrary", "arbitrary"), vmem_limit_bytes=VMEM_LIMIT),
        name="conv1",
    )(h, z, x2, cw, cb, lg, lb, w, g)


def _pad_lanes(v, width=LANES):
    return jnp.pad(v, (0, width - v.shape[0]))


def _even_layer(x2, bsz, seq, g_pre, w_in, conv_w, conv_b, dt_bias, a_log, d_skip, fgate_b,
                ssd_norm, w_out, g_post):
    d_xbc = D_MODEL + 2 * SSD_GROUPS * D_STATE
    o_z, o_xbc = 0, 2 * D_MODEL
    o_dt = o_xbc + d_xbc
    o_q = o_dt + SSD_HEADS
    o_k = o_q + D_MODEL
    o_v = o_k + D_MODEL
    o_f = o_v + D_MODEL
    q_scale = HEADDIM ** -0.5 * LOG2E
    w_main = jnp.concatenate(
        [w_in[:, o_z:o_dt], w_in[:, o_q:o_k] * q_scale, w_in[:, o_k:o_v]],
        axis=1).astype(BF16)
    wvt = w_in[:, o_v:o_f].T.astype(BF16)
    w_small = jnp.concatenate(
        [w_in[:, o_dt:o_q], w_in[:, o_f:o_f + FOX_HEADS],
         jnp.zeros((D_MODEL, LANES - SSD_HEADS - FOX_HEADS), F32)], axis=1)
    wsh = w_small.astype(BF16)
    wsl = (w_small - wsh.astype(F32)).astype(BF16)

    proj, vt, sm, smt = _inproj0(x2, g_pre[None, :], w_main, wvt, wsh, wsl,
                                 tm=TILES.proj_rows, tn=TILES.proj_cols)

    zeros16 = jnp.zeros((SSD_HEADS,), F32)
    prow = jnp.stack([
        _pad_lanes(dt_bias),
        _pad_lanes(a_log),
        _pad_lanes(jnp.concatenate([zeros16, fgate_b])),
    ] + [jnp.zeros((LANES,), F32)] * 5)
    pcol = jnp.stack([jnp.broadcast_to(dt_bias[:, None], (SSD_HEADS, LANES)),
                      jnp.broadcast_to(a_log[:, None], (SSD_HEADS, LANES))])
    dskip_e = jnp.repeat(d_skip, HEADDIM)[None, :]
    head_of_lane = jnp.arange(D_MODEL) // HEADDIM
    e64 = (jnp.arange(LANES)[:, None] == head_of_lane[None, :]).astype(BF16)

    y, c = _ssd(proj, sm, smt, conv_w, conv_b[None, :], prow, pcol, dskip_e,
                ssd_norm[None, :], e64, bsz=bsz, seq=seq, T=TILES.ssd_rows)

    ct = c[:, SSD_HEADS:SSD_HEADS + FOX_HEADS].T.reshape(FOX_HEADS // 2, 2, -1)
    c8 = jnp.pad(ct, ((0, 0), (0, 6), (0, 0)))

    o = _fox(proj, vt, c8, c, bsz=bsz, seq=seq, tq=TILES.fox_q, tk=TILES.fox_kv,
             q_cb0=4 * CB, k_cb0=5 * CB, c_lane0=SSD_HEADS)
    return _outproj0(y, o, proj, x2, w_out.astype(BF16), g_post[None, :], tm=TILES.out_rows)


def _odd_layer(x2, bsz, seq, g_pre, w_in, conv_w, conv_b, ln_g, ln_b, w_out, g_post):
    seg = TILES.glu_seg
    cols = []
    for j in range(D_CONV // seg):
        for part in range(3):
            cols.append(w_in[:, part * D_CONV + j * seg:part * D_CONV + (j + 1) * seg])
    w = jnp.concatenate(cols, axis=1).astype(BF16)
    h, z = _inproj1(x2, g_pre[None, :], w, tm=TILES.proj_rows, seg=seg)
    return _conv1(h, z, x2, conv_w, conv_b[None, :], ln_g[None, :], ln_b[None, :],
                  w_out.astype(BF16), g_post[None, :], bsz=bsz, seq=seq, T=TILES.conv_rows)


def kernel(x, e_norm_pre, e_w_in, e_conv_w, e_conv_b, e_dt_bias, e_a_log, e_d_skip, e_fgate_b,
           e_ssd_norm, e_w_out, e_norm_post, o_norm_pre, o_w_in, o_conv_w, o_conv_b, o_ln_g,
           o_ln_b, o_w_out, o_norm_post):
    bsz, seq, d = x.shape
    x2 = x.reshape(bsz * seq, d)
    depth = e_w_in.shape[0] + o_w_in.shape[0]
    for layer in range(depth):
        i = layer // 2
        if layer % 2 == 0:
            x2 = _even_layer(x2, bsz, seq, e_norm_pre[i], e_w_in[i], e_conv_w[i], e_conv_b[i],
                             e_dt_bias[i], e_a_log[i], e_d_skip[i], e_fgate_b[i],
                             e_ssd_norm[i], e_w_out[i], e_norm_post[i])
        else:
            x2 = _odd_layer(x2, bsz, seq, o_norm_pre[i], o_w_in[i], o_conv_w[i], o_conv_b[i],
                            o_ln_g[i], o_ln_b[i], o_w_out[i], o_norm_post[i])
    return x2.reshape(bsz, seq, d)
```

```python
import functools
import math
from typing import NamedTuple

import jax
import jax.numpy as jnp
from jax import lax
from jax.experimental import pallas as pl
from jax.experimental.pallas import tpu as pltpu

F32 = jnp.float32
BF16 = jnp.bfloat16

LANES = 128
SUBLANES = 8
D_MODEL = 1024
SSD_HEADS = 16
SSD_GROUPS = 4
HEADDIM = 64
D_STATE = 128
CHUNK = 128
SSD_CONV = 4
FOX_HEADS = 16
D_CONV = 2 * D_MODEL
CONV_WIDTH = 31
EPS = 1e-6
NEG_BIG = -1e30
LOG2E = 1.4426950408889634
STRIP = 256
ONES_ROWS = 16
CONV_HALO = 32
LOOKAHEAD = 4
FOX_UNROLL = 4
VMEM_LIMIT = 56 * 1024 * 1024

CB = D_MODEL // LANES


class _Tiles(NamedTuple):
    proj_rows: int = 1024
    proj_cols: int = 2048
    glu_seg: int = 1024
    ssd_rows: int = 256
    fox_q: int = 2048
    fox_kv: int = 512
    out_rows: int = 1024
    conv_rows: int = 512


TILES = _Tiles()


def _dot(a, b):
    return jnp.dot(a, b, preferred_element_type=F32)


def _dot_nt(a, b):
    return lax.dot_general(a, b, (((1,), (1,)), ((), ())), preferred_element_type=F32)


def _dot_tn(a, b):
    return lax.dot_general(a, b, (((0,), (0,)), ((), ())), preferred_element_type=F32)


def _split(a, n):
    pieces = []
    r = a
    for _ in range(n):
        p = r.astype(BF16)
        pieces.append(p)
        r = r - p.astype(F32)
    return pieces


def _dot_split_lhs(a, m_bf16, n):
    out = None
    for p in _split(a, n):
        t = _dot(p, m_bf16)
        out = t if out is None else out + t
    return out


def _dot_split_rhs(m_bf16, a, n):
    out = None
    for p in _split(a, n):
        t = _dot(m_bf16, p)
        out = t if out is None else out + t
    return out


def _sigmoid(x):
    return 1.0 / (1.0 + jnp.exp(-x))


def _silu(x):
    return x * _sigmoid(x)


def _softplus(x):
    return jnp.maximum(x, 0.0) + jnp.log(1.0 + jnp.exp(-jnp.abs(x)))


def _log_sigmoid(x):
    return jnp.minimum(x, 0.0) - jnp.log(1.0 + jnp.exp(-jnp.abs(x)))


def _cat_blocks(ref, lo, hi):
    return jnp.concatenate([ref[c] for c in range(lo, hi)], axis=-1)


def _inproj0_kernel(x_ref, g_ref, w_ref, wvt_ref, wsh_ref, wsl_ref,
                    proj_ref, vt_ref, sm_ref, smt_ref, xn_scr, *, ncb):
    j = pl.program_id(1)

    @pl.when(j == 0)
    def _():
        x = x_ref[...]
        ms = jnp.mean(x * x, axis=-1, keepdims=True)
        xn = x * lax.rsqrt(ms + EPS) * g_ref[...]
        hi = xn.astype(BF16)
        lo = (xn - hi.astype(F32)).astype(BF16)
        xn_scr[...] = hi
        small = _dot(hi, wsh_ref[...]) + _dot(lo, wsh_ref[...]) + _dot(hi, wsl_ref[...])
        sm_ref[...] = small
        smt_ref[...] = small.T
        vt = _dot_nt(wvt_ref[...], hi)
        for c in range(CB):
            vt_ref[c] = vt[c * LANES:(c + 1) * LANES, :].astype(BF16)

    acc = _dot(xn_scr[...], w_ref[...])
    for c in range(ncb):
        proj_ref[c] = acc[:, c * LANES:(c + 1) * LANES].astype(BF16)


def _inproj0(x2, g, w_main, wvt, wsh, wsl, *, tm, tn):
    m = x2.shape[0]
    ncol = w_main.shape[1] // tn
    ncb = tn // LANES
    kern = functools.partial(_inproj0_kernel, ncb=ncb)
    return pl.pallas_call(
        kern,
        grid=(m // tm, ncol),
        in_specs=[
            pl.BlockSpec((tm, D_MODEL), lambda i, j: (i, 0)),
            pl.BlockSpec((1, D_MODEL), lambda i, j: (0, 0)),
            pl.BlockSpec((D_MODEL, tn), lambda i, j: (0, j)),
            pl.BlockSpec((D_MODEL, D_MODEL), lambda i, j: (0, 0)),
            pl.BlockSpec((D_MODEL, LANES), lambda i, j: (0, 0)),
            pl.BlockSpec((D_MODEL, LANES), lambda i, j: (0, 0)),
        ],
        out_specs=[
            pl.BlockSpec((ncb, tm, LANES), lambda i, j: (j, i, 0)),
            pl.BlockSpec((CB, LANES, tm), lambda i, j: (0, 0, i)),
            pl.BlockSpec((tm, LANES), lambda i, j: (i, 0)),
            pl.BlockSpec((LANES, tm), lambda i, j: (0, i)),
        ],
        out_shape=[
            jax.ShapeDtypeStruct((w_main.shape[1] // LANES, m, LANES), BF16),
            jax.ShapeDtypeStruct((CB, LANES, m), BF16),
            jax.ShapeDtypeStruct((m, LANES), F32),
            jax.ShapeDtypeStruct((LANES, m), F32),
        ],
        scratch_shapes=[pltpu.VMEM((tm, D_MODEL), BF16)],
        compiler_params=pltpu.CompilerParams(
            dimension_semantics=("arbitrary", "arbitrary"), vmem_limit_bytes=VMEM_LIMIT),
        name="inproj0",
    )(x2, g, w_main, wvt, wsh, wsl)


def _ssd_kernel(xbc_ref, z_ref, sm_ref, smt_ref, convw_ref, convb_ref, prow_ref, pcol_ref,
                dskip_ref, norm_ref, e64_ref,
                y_ref, c_ref, c8_ref,
                h_scr, ext_scr, xc_scr, ccarry_scr, sh_scr, *, T):
    t = pl.program_id(1)
    nxc = 2 * CB

    @pl.when(t == 0)
    def _():
        h_scr[...] = jnp.zeros_like(h_scr)
        ext_scr[0:8, :] = jnp.zeros((8, nxc * LANES), F32)
        ccarry_scr[...] = jnp.zeros_like(ccarry_scr)

    for c in range(nxc):
        ext_scr[8:8 + T, c * LANES:(c + 1) * LANES] = xbc_ref[c].astype(F32)
    for c in range(nxc):
        cs_ = slice(c * LANES, (c + 1) * LANES)
        for k in range(SSD_CONV - 1):
            off = SUBLANES - (SSD_CONV - 1) + k
            sh_scr[k] = ext_scr[off:off + T, cs_]
        acc = convb_ref[:, cs_] + (ext_scr[SUBLANES:SUBLANES + T, cs_]
                                   * convw_ref[SSD_CONV - 1:SSD_CONV, cs_])
        for k in range(SSD_CONV - 1):
            acc = acc + sh_scr[k] * convw_ref[k:k + 1, cs_]
        xc_scr[:, cs_] = _silu(acc)
    ext_scr[0:8, :] = ext_scr[T:T + 8, :]

    dtb_row = prow_ref[0:1, :]
    a_row = -jnp.exp(prow_ref[1:2, :])
    fb_row = prow_ref[2:3, :]
    dtb_col = pcol_ref[0]
    a_col = -jnp.exp(pcol_ref[1])

    rt = lax.broadcasted_iota(jnp.int32, (T, T), 0)
    ct = lax.broadcasted_iota(jnp.int32, (T, T), 1)
    ltri_t = jnp.where(rt >= ct, 1.0, 0.0).astype(BF16)
    logf = _log_sigmoid(sm_ref[...] + fb_row)
    ctile = _dot_split_rhs(ltri_t, logf, 3) + ccarry_scr[...]
    c_ref[...] = ctile
    ccarry_scr[...] = ctile[T - 1:T, :]
    ctt = ctile.T
    c8_ref[...] = jnp.zeros_like(c8_ref)
    for h in range(FOX_HEADS):
        c8_ref[h // 2, h % 2:h % 2 + 1, :] = ctt[SSD_HEADS + h:SSD_HEADS + h + 1, :]

    ri = lax.broadcasted_iota(jnp.int32, (CHUNK, CHUNK), 0)
    ci = lax.broadcasted_iota(jnp.int32, (CHUNK, CHUNK), 1)
    causal = ri >= ci
    ltri = jnp.where(causal, 1.0, 0.0).astype(BF16)
    utri = jnp.where(ri <= ci, 1.0, 0.0).astype(BF16)
    lane = lax.broadcasted_iota(jnp.int32, (CHUNK, LANES), 1)
    first_head = lane < HEADDIM
    e64 = e64_ref[...]

    for ck in range(T // CHUNK):
        r = slice(ck * CHUNK, (ck + 1) * CHUNK)
        dt = _softplus(sm_ref[r, :] + dtb_row)
        cs = _dot_split_rhs(ltri, dt * a_row, 3)
        dtt = _softplus(smt_ref[0:SSD_HEADS, r] + dtb_col)
        cst = _dot_split_lhs(dtt * a_col, utri, 3)
        cs_e = _dot_split_lhs(cs, e64, 3)
        dt_e = _dot_split_lhs(dt, e64, 3)
        ecs_e = jnp.exp(cs_e)
        dte_e = jnp.exp(cs_e[CHUNK - 1:CHUNK, :] - cs_e)

        for g in range(SSD_GROUPS):
            bsl = slice(D_MODEL + g * D_STATE, D_MODEL + (g + 1) * D_STATE)
            csl = slice(D_MODEL + (SSD_GROUPS + g) * D_STATE,
                        D_MODEL + (SSD_GROUPS + g + 1) * D_STATE)
            bg = xc_scr[r, bsl].astype(BF16)
            cg = xc_scr[r, csl].astype(BF16)
            cb_ = _dot_nt(cg, bg)
            ys = []
            for pp in range(2):
                p = 2 * g + pp
                ps = slice(p * LANES, (p + 1) * LANES)
                x = xc_scr[r, ps]
                xd = x * dt_e[:, ps]
                xd_b = xd.astype(BF16)
                yd = []
                for hh in range(2):
                    h = 2 * p + hh
                    colb = jnp.broadcast_to(cs[:, h:h + 1], (CHUNK, CHUNK))
                    rowb = jnp.broadcast_to(cst[h:h + 1, :], (CHUNK, CHUNK))
                    lm = jnp.where(causal, jnp.exp(colb - rowb), 0.0)
                    yd.append(_dot((cb_ * lm).astype(BF16), xd_b))
                ydiag = jnp.where(first_head, yd[0], yd[1])
                hin = h_scr[:, ps]
                yoff = _dot(cg, hin.astype(BF16)) * ecs_e[:, ps]
                snew = _dot_tn(bg, (xd * dte_e[:, ps]).astype(BF16))
                h_scr[:, ps] = hin * ecs_e[CHUNK - 1:CHUNK, ps] + snew
                yp = ydiag + yoff + x * dskip_ref[:, ps]
                yp = yp * _silu(z_ref[p, r, :].astype(F32))
                ys.append(yp)
            ss = (jnp.sum(ys[0] * ys[0], axis=-1, keepdims=True)
                  + jnp.sum(ys[1] * ys[1], axis=-1, keepdims=True))
            scale = lax.rsqrt(ss * (1.0 / (2 * LANES)) + EPS)
            for pp in range(2):
                p = 2 * g + pp
                ps = slice(p * LANES, (p + 1) * LANES)
                y_ref[p, r, :] = (ys[pp] * scale * norm_ref[:, ps]).astype(BF16)


def _ssd(proj, sm, smt, convw, convb, prow, pcol, dskip_e, norm, e64, *, bsz, seq, T):
    m = bsz * seq
    nt = seq // T
    kern = functools.partial(_ssd_kernel, T=T)
    const2 = lambda b, t: (0, 0)
    return pl.pallas_call(
        kern,
        grid=(bsz, nt),
        in_specs=[
            pl.BlockSpec((2 * CB, T, LANES), lambda b, t: (1, b * nt + t, 0)),
            pl.BlockSpec((CB, T, LANES), lambda b, t: (0, b * nt + t, 0)),
            pl.BlockSpec((T, LANES), lambda b, t: (b * nt + t, 0)),
            pl.BlockSpec((LANES, T), lambda b, t: (0, b * nt + t)),
            pl.BlockSpec(convw.shape, const2),
            pl.BlockSpec(convb.shape, const2),
            pl.BlockSpec(prow.shape, const2),
            pl.BlockSpec(pcol.shape, lambda b, t: (0, 0, 0)),
            pl.BlockSpec(dskip_e.shape, const2),
            pl.BlockSpec(norm.shape, const2),
            pl.BlockSpec(e64.shape, const2),
        ],
        out_specs=[
            pl.BlockSpec((CB, T, LANES), lambda b, t: (0, b * nt + t, 0)),
            pl.BlockSpec((T, LANES), lambda b, t: (b * nt + t, 0)),
            pl.BlockSpec((FOX_HEADS // 2, SUBLANES, T), lambda b, t: (0, 0, b * nt + t)),
        ],
        out_shape=[
            jax.ShapeDtypeStruct((CB, m, LANES), BF16),
            jax.ShapeDtypeStruct((m, LANES), F32),
            jax.ShapeDtypeStruct((FOX_HEADS // 2, SUBLANES, m), F32),
        ],
        scratch_shapes=[
            pltpu.VMEM((D_STATE, D_MODEL), F32),
            pltpu.VMEM((T + 8, 2 * D_MODEL), F32),
            pltpu.VMEM((T, 2 * D_MODEL), F32),
            pltpu.VMEM((1, LANES), F32),
            pltpu.VMEM((SSD_CONV - 1, T, LANES), F32),
        ],
        compiler_params=pltpu.CompilerParams(
            dimension_semantics=("arbitrary", "arbitrary"), vmem_limit_bytes=VMEM_LIMIT),
        name="ssd",
    )(proj, proj, sm, smt, convw, convb, prow, pcol, dskip_e, norm, e64)


def _fox_kernel(q_ref, k_ref, vt_ref, cq_ref, c_ref, o_ref, qh_scr, m_scr, acc_scr, *,
                tq, tk, c_lane0):
    pair = pl.program_id(1)
    i = pl.program_id(2)
    nstrip = tq // STRIP
    lane_k = lax.broadcasted_iota(jnp.int32, (tk, LANES), 1)
    lane = lax.broadcasted_iota(jnp.int32, (tq, LANES), 1)
    q = q_ref[...]
    zero = jnp.zeros_like(q)
    qh_scr[0] = jnp.where(lane < HEADDIM, q, zero)
    qh_scr[1] = jnp.where(lane >= HEADDIM, q, zero)
    m_scr[...] = jnp.full_like(m_scr, NEG_BIG)
    acc_scr[...] = jnp.zeros_like(acc_scr)
    ones = jnp.ones((ONES_ROWS, tk), BF16)

    def scores(hh, st, nkv, k):
        return _dot_nt(k[0:nkv], qh_scr[hh, st * STRIP:(st + 1) * STRIP, :])

    def softmax_pv(hh, st, nkv, s, ck2, vta, koff, partial):
        cols = slice(st * STRIP, (st + 1) * STRIP)
        t = s - ck2[0:nkv]
        vta = vta[:, 0:nkv]
        if partial:
            kv_pos = koff + lax.broadcasted_iota(jnp.int32, (nkv, STRIP), 0)
            q_pos = i * tq + st * STRIP + lax.broadcasted_iota(jnp.int32, (nkv, STRIP), 1)
            t = jnp.where(kv_pos <= q_pos, t, NEG_BIG)
        cq2 = cq_ref[hh:hh + 1, cols] * LOG2E
        m_prev = m_scr[hh, :, cols]
        m_new = jnp.maximum(m_prev, jnp.max(t, axis=0, keepdims=True) + cq2)
        alpha = jnp.exp2(m_prev - m_new)
        p = jnp.exp2((t - (m_new - cq2)).astype(BF16))
        acc_scr[hh, :, cols] = acc_scr[hh, :, cols] * alpha + _dot(vta, p)
        m_scr[hh, :, cols] = m_new

    def strips_of(diag):
        items = []
        for hh in range(2):
            for st in range(nstrip):
                nkv, partial = tk, False
                if diag is not None:
                    kv_lo, q_lo = diag * tk, st * STRIP
                    if kv_lo > q_lo + STRIP - 1:
                        continue
                    nkv = min(tk, q_lo + STRIP - kv_lo)
                    partial = kv_lo + nkv - 1 > q_lo
                items.append((hh, st, nkv, partial))
        return items

    def blocks(j0, diags):
        koffs, ks, ck2s, vtas, items = [], [], [], [], []
        for b, diag in enumerate(diags):
            koff = pl.multiple_of((j0 + b) * tk, tk)
            cblk = c_ref[pl.ds(koff, tk), :]
            ck2, vta = [], []
            for hh in range(2):
                sel = lane_k == (c_lane0 + 2 * pair + hh)
                ck2.append(jnp.sum(jnp.where(sel, cblk, 0.0), axis=1, keepdims=True) * LOG2E)
                vta.append(jnp.concatenate(
                    [vt_ref[hh * HEADDIM:(hh + 1) * HEADDIM, pl.ds(koff, tk)], ones], axis=0))
            koffs.append(koff)
            ks.append(k_ref[pl.ds(koff, tk), :])
            ck2s.append(ck2)
            vtas.append(vta)
            items += [(b,) + it for it in strips_of(diag)]
        pending = [scores(hh, st, nkv, ks[b]) for b, hh, st, nkv, _ in items[:LOOKAHEAD]]
        for n, (b, hh, st, nkv, partial) in enumerate(items):
            if n + LOOKAHEAD < len(items):
                nb, nhh, nst, nnkv, _ = items[n + LOOKAHEAD]
                pending.append(scores(nhh, nst, nnkv, ks[nb]))
            softmax_pv(hh, st, nkv, pending.pop(0), ck2s[b][hh], vtas[b][hh], koffs[b], partial)

    ndiag = tq // tk
    nfull = i * ndiag
    unroll = math.gcd(ndiag, FOX_UNROLL)

    def body(j, carry):
        blocks(j * unroll, [None] * unroll)
        return carry

    lax.fori_loop(0, nfull // unroll, body, 0)
    blocks(nfull, list(range(ndiag)))

    outs = []
    for hh in range(2):
        acc = acc_scr[hh]
        outs.append(acc[0:HEADDIM, :] * (1.0 / acc[HEADDIM:HEADDIM + 1, :]))
    o_ref[...] = jnp.concatenate(outs, axis=0).T.astype(BF16)


def _fox(proj, vt, c8, c, *, bsz, seq, tq, tk, q_cb0, k_cb0, c_lane0):
    m = bsz * seq
    nq = seq // tq
    npair = FOX_HEADS // 2
    kern = functools.partial(_fox_kernel, tq=tq, tk=tk, c_lane0=c_lane0)
    return pl.pallas_call(
        kern,
        grid=(bsz, npair, nq),
        in_specs=[
            pl.BlockSpec((None, tq, LANES), lambda b, p, i: (q_cb0 + p, b * nq + i, 0)),
            pl.BlockSpec((None, seq, LANES), lambda b, p, i: (k_cb0 + p, b, 0)),
            pl.BlockSpec((None, LANES, seq), lambda b, p, i: (p, 0, b)),
            pl.BlockSpec((None, 8, tq), lambda b, p, i: (p, 0, b * nq + i)),
            pl.BlockSpec((seq, LANES), lambda b, p, i: (b, 0)),
        ],
        out_specs=pl.BlockSpec((None, tq, LANES), lambda b, p, i: (p, b * nq + i, 0)),
        out_shape=jax.ShapeDtypeStruct((npair, m, LANES), BF16),
        scratch_shapes=[
            pltpu.VMEM((2, tq, LANES), BF16),
            pltpu.VMEM((2, 1, tq), F32),
            pltpu.VMEM((2, HEADDIM + ONES_ROWS, tq), F32),
        ],
        compiler_params=pltpu.CompilerParams(
            dimension_semantics=("arbitrary", "arbitrary", "arbitrary"),
            vmem_limit_bytes=VMEM_LIMIT),
        name="fox",
    )(proj, proj, vt, c8, c)


def _outproj0_kernel(y_ref, o_ref, z_ref, x_ref, w_ref, g_ref, out_ref):
    y = _cat_blocks(y_ref, 0, CB)
    o = _cat_blocks(o_ref, 0, CB).astype(F32)
    z = _cat_blocks(z_ref, 0, CB).astype(F32)
    og = (o * _silu(z)).astype(BF16)
    acc = _dot(y, w_ref[0:D_MODEL, :]) + _dot(og, w_ref[D_MODEL:2 * D_MODEL, :])
    ms = jnp.mean(acc * acc, axis=-1, keepdims=True)
    out_ref[...] = x_ref[...] + acc * lax.rsqrt(ms + EPS) * g_ref[...]


def _outproj0(y, o, proj, x2, w, g, *, tm):
    m = x2.shape[0]
    return pl.pallas_call(
        _outproj0_kernel,
        grid=(m // tm,),
        in_specs=[
            pl.BlockSpec((CB, tm, LANES), lambda i: (0, i, 0)),
            pl.BlockSpec((CB, tm, LANES), lambda i: (0, i, 0)),
            pl.BlockSpec((CB, tm, LANES), lambda i: (1, i, 0)),
            pl.BlockSpec((tm, D_MODEL), lambda i: (i, 0)),
            pl.BlockSpec(w.shape, lambda i: (0, 0)),
            pl.BlockSpec((1, D_MODEL), lambda i: (0, 0)),
        ],
        out_specs=pl.BlockSpec((tm, D_MODEL), lambda i: (i, 0)),
        out_shape=jax.ShapeDtypeStruct((m, D_MODEL), F32),
        compiler_params=pltpu.CompilerParams(
            dimension_semantics=("arbitrary",), vmem_limit_bytes=VMEM_LIMIT),
        name="outproj0",
    )(y, o, proj, x2, w, g)


def _inproj1_kernel(x_ref, g_ref, w_ref, h_ref, z_ref, xn_scr, *, seg):
    j = pl.program_id(1)

    @pl.when(j == 0)
    def _():
        x = x_ref[...]
        ms = jnp.mean(x * x, axis=-1, keepdims=True)
        xn_scr[...] = (x * lax.rsqrt(ms + EPS) * g_ref[...]).astype(BF16)

    acc = _dot(xn_scr[...], w_ref[...])
    hv = acc[:, 0:seg] * _sigmoid(acc[:, seg:2 * seg])
    for c in range(seg // LANES):
        h_ref[c] = hv[:, c * LANES:(c + 1) * LANES].astype(BF16)
        z_ref[c] = acc[:, 2 * seg + c * LANES:2 * seg + (c + 1) * LANES].astype(BF16)


def _inproj1(x2, g, w, *, tm, seg):
    m = x2.shape[0]
    tn = 3 * seg
    ncol = w.shape[1] // tn
    scb = seg // LANES
    kern = functools.partial(_inproj1_kernel, seg=seg)
    return pl.pallas_call(
        kern,
        grid=(m // tm, ncol),
        in_specs=[
            pl.BlockSpec((tm, D_MODEL), lambda i, j: (i, 0)),
            pl.BlockSpec((1, D_MODEL), lambda i, j: (0, 0)),
            pl.BlockSpec((D_MODEL, tn), lambda i, j: (0, j)),
        ],
        out_specs=[
            pl.BlockSpec((scb, tm, LANES), lambda i, j: (j, i, 0)),
            pl.BlockSpec((scb, tm, LANES), lambda i, j: (j, i, 0)),
        ],
        out_shape=[
            jax.ShapeDtypeStruct((D_CONV // LANES, m, LANES), BF16),
            jax.ShapeDtypeStruct((D_CONV // LANES, m, LANES), BF16),
        ],
        scratch_shapes=[pltpu.VMEM((tm, D_MODEL), BF16)],
        compiler_params=pltpu.CompilerParams(
            dimension_semantics=("arbitrary", "arbitrary"), vmem_limit_bytes=VMEM_LIMIT),
        name="inproj1",
    )(x2, g, w)


def _conv1_kernel(h_ref, z_ref, x_ref, cw_ref, cb_ref, lg_ref, lb_ref, w_ref, g_ref,
                  out_ref, ext_scr, hc_scr, sh_scr, *, T):
    t = pl.program_id(1)
    ncb = D_CONV // LANES
    halo = CONV_HALO

    @pl.when(t == 0)
    def _():
        ext_scr[0:halo, :] = jnp.zeros((halo, D_CONV), F32)

    for c in range(ncb):
        ext_scr[halo:halo + T, c * LANES:(c + 1) * LANES] = h_ref[c].astype(F32)
    span = T + halo - SUBLANES
    for c in range(ncb):
        cs_ = slice(c * LANES, (c + 1) * LANES)
        for r in range(1, SUBLANES):
            sh_scr[r - 1] = ext_scr[r:r + span, cs_]
        acc = cb_ref[:, cs_]
        for k in range(CONV_WIDTH):
            base, r = divmod(halo - (CONV_WIDTH - 1) + k, SUBLANES)
            base *= SUBLANES
            if r == 0:
                src = ext_scr[base:base + T, cs_]
            else:
                src = sh_scr[r - 1, base:base + T, :]
            acc = acc + src * cw_ref[k:k + 1, cs_]
        hc_scr[:, cs_] = acc
    ext_scr[0:halo, :] = ext_scr[T:T + halo, :]

    hc = hc_scr[...]
    mu = jnp.mean(hc, axis=-1, keepdims=True)
    xc = hc - mu
    var = jnp.mean(xc * xc, axis=-1, keepdims=True)
    hn = xc * lax.rsqrt(var + EPS) * lg_ref[...] + lb_ref[...]
    z = _cat_blocks(z_ref, 0, ncb).astype(F32)
    hg = (_silu(hn) * _silu(z)).astype(BF16)
    acc = _dot(hg, w_ref[...])
    ms = jnp.mean(acc * acc, axis=-1, keepdims=True)
    out_ref[...] = x_ref[...] + acc * lax.rsqrt(ms + EPS) * g_ref[...]


def _conv1(h, z, x2, cw, cb, lg, lb, w, g, *, bsz, seq, T):
    m = bsz * seq
    nt = seq // T
    ncb = D_CONV // LANES
    kern = functools.partial(_conv1_kernel, T=T)
    const2 = lambda b, t: (0, 0)
    return pl.pallas_call(
        kern,
        grid=(bsz, nt),
        in_specs=[
            pl.BlockSpec((ncb, T, LANES), lambda b, t: (0, b * nt + t, 0)),
            pl.BlockSpec((ncb, T, LANES), lambda b, t: (0, b * nt + t, 0)),
            pl.BlockSpec((T, D_MODEL), lambda b, t: (b * nt + t, 0)),
            pl.BlockSpec(cw.shape, const2),
            pl.BlockSpec(cb.shape, const2),
            pl.BlockSpec(lg.shape, const2),
            pl.BlockSpec(lb.shape, const2),
            pl.BlockSpec(w.shape, const2),
            pl.BlockSpec(g.shape, const2),
        ],
        out_specs=pl.BlockSpec((T, D_MODEL), lambda b, t: (b * nt + t, 0)),
        out_shape=jax.ShapeDtypeStruct((m, D_MODEL), F32),
        scratch_shapes=[
            pltpu.VMEM((T + CONV_HALO, D_CONV), F32),
            pltpu.VMEM((T, D_CONV), F32),
            pltpu.VMEM((SUBLANES - 1, T + CONV_HALO - SUBLANES, LANES), F32),
        ],
        compiler_params=pltpu.CompilerParams(
            dimension_semantics=("arbitrary", "arbitrary"), vmem_limit_bytes=VMEM_LIMIT),
        name="conv1",
    )(h, z, x2, cw, cb, lg, lb, w, g)


def _pad_lanes(v, width=LANES):
    return jnp.pad(v, (0, width - v.shape[0]))


def _even_layer(x2, bsz, seq, g_pre, w_in, conv_w, conv_b, dt_bias, a_log, d_skip, fgate_b,
                ssd_norm, w_out, g_post):
    d_xbc = D_MODEL + 2 * SSD_GROUPS * D_STATE
    o_z, o_xbc = 0, 2 * D_MODEL
    o_dt = o_xbc + d_xbc
    o_q = o_dt + SSD_HEADS
    o_k = o_q + D_MODEL
    o_v = o_k + D_MODEL
    o_f = o_v + D_MODEL
    q_scale = HEADDIM ** -0.5 * LOG2E
    w_main = jnp.concatenate(
        [w_in[:, o_z:o_dt], w_in[:, o_q:o_k] * q_scale, w_in[:, o_k:o_v]],
        axis=1).astype(BF16)
    wvt = w_in[:, o_v:o_f].T.astype(BF16)
    w_small = jnp.concatenate(
        [w_in[:, o_dt:o_q], w_in[:, o_f:o_f + FOX_HEADS],
         jnp.zeros((D_MODEL, LANES - SSD_HEADS - FOX_HEADS), F32)], axis=1)
    wsh = w_small.astype(BF16)
    wsl = (w_small - wsh.astype(F32)).astype(BF16)

    proj, vt, sm, smt = _inproj0(x2, g_pre[None, :], w_main, wvt, wsh, wsl,
                                 tm=TILES.proj_rows, tn=TILES.proj_cols)

    zeros16 = jnp.zeros((SSD_HEADS,), F32)
    prow = jnp.stack([
        _pad_lanes(dt_bias),
        _pad_lanes(a_log),
        _pad_lanes(jnp.concatenate([zeros16, fgate_b])),
    ] + [jnp.zeros((LANES,), F32)] * 5)
    pcol = jnp.stack([jnp.broadcast_to(dt_bias[:, None], (SSD_HEADS, LANES)),
                      jnp.broadcast_to(a_log[:, None], (SSD_HEADS, LANES))])
    dskip_e = jnp.repeat(d_skip, HEADDIM)[None, :]
    head_of_lane = jnp.arange(D_MODEL) // HEADDIM
    e64 = (jnp.arange(LANES)[:, None] == head_of_lane[None, :]).astype(BF16)

    y, c, c8 = _ssd(proj, sm, smt, conv_w, conv_b[None, :], prow, pcol, dskip_e,
                    ssd_norm[None, :], e64, bsz=bsz, seq=seq, T=TILES.ssd_rows)

    o = _fox(proj, vt, c8, c, bsz=bsz, seq=seq, tq=TILES.fox_q, tk=TILES.fox_kv,
             q_cb0=4 * CB, k_cb0=5 * CB, c_lane0=SSD_HEADS)
    return _outproj0(y, o, proj, x2, w_out.astype(BF16), g_post[None, :], tm=TILES.out_rows)


def _odd_layer(x2, bsz, seq, g_pre, w_in, conv_w, conv_b, ln_g, ln_b, w_out, g_post):
    seg = TILES.glu_seg
    cols = []
    for j in range(D_CONV // seg):
        for part in range(3):
            cols.append(w_in[:, part * D_CONV + j * seg:part * D_CONV + (j + 1) * seg])
    w = jnp.concatenate(cols, axis=1).astype(BF16)
    h, z = _inproj1(x2, g_pre[None, :], w, tm=TILES.proj_rows, seg=seg)
    return _conv1(h, z, x2, conv_w, conv_b[None, :], ln_g[None, :], ln_b[None, :],
                  w_out.astype(BF16), g_post[None, :], bsz=bsz, seq=seq, T=TILES.conv_rows)


def kernel(x, e_norm_pre, e_w_in, e_conv_w, e_conv_b, e_dt_bias, e_a_log, e_d_skip, e_fgate_b,
           e_ssd_norm, e_w_out, e_norm_post, o_norm_pre, o_w_in, o_conv_w, o_conv_b, o_ln_g,
           o_ln_b, o_w_out, o_norm_post):
    bsz, seq, d = x.shape
    x2 = x.reshape(bsz * seq, d)
    depth = e_w_in.shape[0] + o_w_in.shape[0]
    for layer in range(depth):
        i = layer // 2
        if layer % 2 == 0:
            x2 = _even_layer(x2, bsz, seq, e_norm_pre[i], e_w_in[i], e_conv_w[i], e_conv_b[i],
                             e_dt_bias[i], e_a_log[i], e_d_skip[i], e_fgate_b[i],
                             e_ssd_norm[i], e_w_out[i], e_norm_post[i])
        else:
            x2 = _odd_layer(x2, bsz, seq, o_norm_pre[i], o_w_in[i], o_conv_w[i], o_conv_b[i],
                            o_ln_g[i], o_ln_b[i], o_w_out[i], o_norm_post[i])
    return x2.reshape(bsz, seq, d)
```

```python
import functools
import math
from typing import NamedTuple

import jax
import jax.numpy as jnp
from jax import lax
from jax.experimental import pallas as pl
from jax.experimental.pallas import tpu as pltpu

F32 = jnp.float32
BF16 = jnp.bfloat16

LANES = 128
SUBLANES = 8
D_MODEL = 1024
SSD_HEADS = 16
SSD_GROUPS = 4
HEADDIM = 64
D_STATE = 128
CHUNK = 128
SSD_CONV = 4
FOX_HEADS = 16
D_CONV = 2 * D_MODEL
CONV_WIDTH = 31
EPS = 1e-6
NEG_BIG = -1e30
LOG2E = 1.4426950408889634
STRIP = 256
ONES_ROWS = 16
CONV_HALO = 32
EPI_ROWS = 256
LOOKAHEAD = 4
FOX_UNROLL = 4
VMEM_LIMIT = 56 * 1024 * 1024

CB = D_MODEL // LANES


class _Tiles(NamedTuple):
    proj_rows: int = 1024
    proj_cols: int = 2048
    glu_seg: int = 1024
    ssd_rows: int = 256
    fox_q: int = 2048
    fox_kv: int = 512
    out_rows: int = 1024
    conv_rows: int = 512


TILES = _Tiles()


def _dot(a, b):
    return jnp.dot(a, b, preferred_element_type=F32)


def _dot_nt(a, b):
    return lax.dot_general(a, b, (((1,), (1,)), ((), ())), preferred_element_type=F32)


def _dot_tn(a, b):
    return lax.dot_general(a, b, (((0,), (0,)), ((), ())), preferred_element_type=F32)


def _split(a, n):
    pieces = []
    r = a
    for _ in range(n):
        p = r.astype(BF16)
        pieces.append(p)
        r = r - p.astype(F32)
    return pieces


def _dot_split_lhs(a, m_bf16, n):
    out = None
    for p in _split(a, n):
        t = _dot(p, m_bf16)
        out = t if out is None else out + t
    return out


def _dot_split_rhs(m_bf16, a, n):
    out = None
    for p in _split(a, n):
        t = _dot(m_bf16, p)
        out = t if out is None else out + t
    return out


def _sigmoid(x):
    return 1.0 / (1.0 + jnp.exp(-x))


def _silu(x):
    return x * _sigmoid(x)


def _softplus(x):
    return jnp.maximum(x, 0.0) + jnp.log(1.0 + jnp.exp(-jnp.abs(x)))


def _log_sigmoid(x):
    return jnp.minimum(x, 0.0) - jnp.log(1.0 + jnp.exp(-jnp.abs(x)))


def _cat_blocks(ref, lo, hi):
    return jnp.concatenate([ref[c] for c in range(lo, hi)], axis=-1)


def _inproj0_kernel(x_ref, g_ref, w_ref, wvt_ref, wsh_ref, wsl_ref,
                    proj_ref, vt_ref, sm_ref, smt_ref, xn_scr, *, ncb):
    j = pl.program_id(1)

    @pl.when(j == 0)
    def _():
        x = x_ref[...]
        ms = jnp.mean(x * x, axis=-1, keepdims=True)
        xn = x * lax.rsqrt(ms + EPS) * g_ref[...]
        hi = xn.astype(BF16)
        lo = (xn - hi.astype(F32)).astype(BF16)
        xn_scr[...] = hi
        small = _dot(hi, wsh_ref[...]) + _dot(lo, wsh_ref[...]) + _dot(hi, wsl_ref[...])
        sm_ref[...] = small
        smt_ref[...] = small.T
        vt = _dot_nt(wvt_ref[...], hi)
        for c in range(CB):
            vt_ref[c] = vt[c * LANES:(c + 1) * LANES, :].astype(BF16)

    acc = _dot(xn_scr[...], w_ref[...])
    for c in range(ncb):
        proj_ref[c] = acc[:, c * LANES:(c + 1) * LANES].astype(BF16)


def _inproj0(x2, g, w_main, wvt, wsh, wsl, *, tm, tn):
    m = x2.shape[0]
    ncol = w_main.shape[1] // tn
    ncb = tn // LANES
    kern = functools.partial(_inproj0_kernel, ncb=ncb)
    return pl.pallas_call(
        kern,
        grid=(m // tm, ncol),
        in_specs=[
            pl.BlockSpec((tm, D_MODEL), lambda i, j: (i, 0)),
            pl.BlockSpec((1, D_MODEL), lambda i, j: (0, 0)),
            pl.BlockSpec((D_MODEL, tn), lambda i, j: (0, j)),
            pl.BlockSpec((D_MODEL, D_MODEL), lambda i, j: (0, 0)),
            pl.BlockSpec((D_MODEL, LANES), lambda i, j: (0, 0)),
            pl.BlockSpec((D_MODEL, LANES), lambda i, j: (0, 0)),
        ],
        out_specs=[
            pl.BlockSpec((ncb, tm, LANES), lambda i, j: (j, i, 0)),
            pl.BlockSpec((CB, LANES, tm), lambda i, j: (0, 0, i)),
            pl.BlockSpec((tm, LANES), lambda i, j: (i, 0)),
            pl.BlockSpec((LANES, tm), lambda i, j: (0, i)),
        ],
        out_shape=[
            jax.ShapeDtypeStruct((w_main.shape[1] // LANES, m, LANES), BF16),
            jax.ShapeDtypeStruct((CB, LANES, m), BF16),
            jax.ShapeDtypeStruct((m, LANES), F32),
            jax.ShapeDtypeStruct((LANES, m), F32),
        ],
        scratch_shapes=[pltpu.VMEM((tm, D_MODEL), BF16)],
        compiler_params=pltpu.CompilerParams(
            dimension_semantics=("arbitrary", "arbitrary"), vmem_limit_bytes=VMEM_LIMIT),
        name="inproj0",
    )(x2, g, w_main, wvt, wsh, wsl)


def _ssd_kernel(xbc_ref, z_ref, sm_ref, smt_ref, convw_ref, convb_ref, prow_ref, pcol_ref,
                dskip_ref, norm_ref, e64_ref,
                y_ref, c_ref, c8_ref,
                h_scr, ext_scr, xc_scr, ccarry_scr, sh_scr, *, T):
    t = pl.program_id(1)
    nxc = 2 * CB

    @pl.when(t == 0)
    def _():
        h_scr[...] = jnp.zeros_like(h_scr)
        ext_scr[0:8, :] = jnp.zeros((8, nxc * LANES), F32)
        ccarry_scr[...] = jnp.zeros_like(ccarry_scr)

    for c in range(nxc):
        ext_scr[8:8 + T, c * LANES:(c + 1) * LANES] = xbc_ref[c].astype(F32)
    for c in range(nxc):
        cs_ = slice(c * LANES, (c + 1) * LANES)
        for k in range(SSD_CONV - 1):
            off = SUBLANES - (SSD_CONV - 1) + k
            sh_scr[k] = ext_scr[off:off + T, cs_]
        acc = convb_ref[:, cs_] + (ext_scr[SUBLANES:SUBLANES + T, cs_]
                                   * convw_ref[SSD_CONV - 1:SSD_CONV, cs_])
        for k in range(SSD_CONV - 1):
            acc = acc + sh_scr[k] * convw_ref[k:k + 1, cs_]
        xc_scr[:, cs_] = _silu(acc)
    ext_scr[0:8, :] = ext_scr[T:T + 8, :]

    dtb_row = prow_ref[0:1, :]
    a_row = -jnp.exp(prow_ref[1:2, :])
    fb_row = prow_ref[2:3, :]
    dtb_col = pcol_ref[0]
    a_col = -jnp.exp(pcol_ref[1])

    rt = lax.broadcasted_iota(jnp.int32, (T, T), 0)
    ct = lax.broadcasted_iota(jnp.int32, (T, T), 1)
    ltri_t = jnp.where(rt >= ct, 1.0, 0.0).astype(BF16)
    logf = _log_sigmoid(sm_ref[...] + fb_row)
    ctile = _dot_split_rhs(ltri_t, logf, 3) + ccarry_scr[...]
    c_ref[...] = ctile
    ccarry_scr[...] = ctile[T - 1:T, :]
    ctt = ctile.T
    c8_ref[...] = jnp.zeros_like(c8_ref)
    for h in range(FOX_HEADS):
        c8_ref[h // 2, h % 2:h % 2 + 1, :] = ctt[SSD_HEADS + h:SSD_HEADS + h + 1, :]

    ri = lax.broadcasted_iota(jnp.int32, (CHUNK, CHUNK), 0)
    ci = lax.broadcasted_iota(jnp.int32, (CHUNK, CHUNK), 1)
    causal = ri >= ci
    ltri = jnp.where(causal, 1.0, 0.0).astype(BF16)
    utri = jnp.where(ri <= ci, 1.0, 0.0).astype(BF16)
    lane = lax.broadcasted_iota(jnp.int32, (CHUNK, LANES), 1)
    first_head = lane < HEADDIM
    e64 = e64_ref[...]

    for ck in range(T // CHUNK):
        r = slice(ck * CHUNK, (ck + 1) * CHUNK)
        dt = _softplus(sm_ref[r, :] + dtb_row)
        cs = _dot_split_rhs(ltri, dt * a_row, 3)
        dtt = _softplus(smt_ref[0:SSD_HEADS, r] + dtb_col)
        cst = _dot_split_lhs(dtt * a_col, utri, 3)
        cs_e = _dot_split_lhs(cs, e64, 3)
        dt_e = _dot_split_lhs(dt, e64, 3)
        ecs_e = jnp.exp(cs_e)
        dte_e = jnp.exp(cs_e[CHUNK - 1:CHUNK, :] - cs_e)

        for g in range(SSD_GROUPS):
            bsl = slice(D_MODEL + g * D_STATE, D_MODEL + (g + 1) * D_STATE)
            csl = slice(D_MODEL + (SSD_GROUPS + g) * D_STATE,
                        D_MODEL + (SSD_GROUPS + g + 1) * D_STATE)
            bg = xc_scr[r, bsl].astype(BF16)
            cg = xc_scr[r, csl].astype(BF16)
            cb_ = _dot_nt(cg, bg)
            ys = []
            for pp in range(2):
                p = 2 * g + pp
                ps = slice(p * LANES, (p + 1) * LANES)
                x = xc_scr[r, ps]
                xd = x * dt_e[:, ps]
                xd_b = xd.astype(BF16)
                yd = []
                for hh in range(2):
                    h = 2 * p + hh
                    colb = jnp.broadcast_to(cs[:, h:h + 1], (CHUNK, CHUNK))
                    rowb = jnp.broadcast_to(cst[h:h + 1, :], (CHUNK, CHUNK))
                    lm = jnp.where(causal, jnp.exp(colb - rowb), 0.0)
                    yd.append(_dot((cb_ * lm).astype(BF16), xd_b))
                ydiag = jnp.where(first_head, yd[0], yd[1])
                hin = h_scr[:, ps]
                yoff = _dot(cg, hin.astype(BF16)) * ecs_e[:, ps]
                snew = _dot_tn(bg, (xd * dte_e[:, ps]).astype(BF16))
                h_scr[:, ps] = hin * ecs_e[CHUNK - 1:CHUNK, ps] + snew
                yp = ydiag + yoff + x * dskip_ref[:, ps]
                yp = yp * _silu(z_ref[p, r, :].astype(F32))
                ys.append(yp)
            ss = (jnp.sum(ys[0] * ys[0], axis=-1, keepdims=True)
                  + jnp.sum(ys[1] * ys[1], axis=-1, keepdims=True))
            scale = lax.rsqrt(ss * (1.0 / (2 * LANES)) + EPS)
            for pp in range(2):
                p = 2 * g + pp
                ps = slice(p * LANES, (p + 1) * LANES)
                y_ref[p, r, :] = (ys[pp] * scale * norm_ref[:, ps]).astype(BF16)


def _ssd(proj, sm, smt, convw, convb, prow, pcol, dskip_e, norm, e64, *, bsz, seq, T):
    m = bsz * seq
    nt = seq // T
    kern = functools.partial(_ssd_kernel, T=T)
    const2 = lambda b, t: (0, 0)
    return pl.pallas_call(
        kern,
        grid=(bsz, nt),
        in_specs=[
            pl.BlockSpec((2 * CB, T, LANES), lambda b, t: (1, b * nt + t, 0)),
            pl.BlockSpec((CB, T, LANES), lambda b, t: (0, b * nt + t, 0)),
            pl.BlockSpec((T, LANES), lambda b, t: (b * nt + t, 0)),
            pl.BlockSpec((LANES, T), lambda b, t: (0, b * nt + t)),
            pl.BlockSpec(convw.shape, const2),
            pl.BlockSpec(convb.shape, const2),
            pl.BlockSpec(prow.shape, const2),
            pl.BlockSpec(pcol.shape, lambda b, t: (0, 0, 0)),
            pl.BlockSpec(dskip_e.shape, const2),
            pl.BlockSpec(norm.shape, const2),
            pl.BlockSpec(e64.shape, const2),
        ],
        out_specs=[
            pl.BlockSpec((CB, T, LANES), lambda b, t: (0, b * nt + t, 0)),
            pl.BlockSpec((T, LANES), lambda b, t: (b * nt + t, 0)),
            pl.BlockSpec((FOX_HEADS // 2, SUBLANES, T), lambda b, t: (0, 0, b * nt + t)),
        ],
        out_shape=[
            jax.ShapeDtypeStruct((CB, m, LANES), BF16),
            jax.ShapeDtypeStruct((m, LANES), F32),
            jax.ShapeDtypeStruct((FOX_HEADS // 2, SUBLANES, m), F32),
        ],
        scratch_shapes=[
            pltpu.VMEM((D_STATE, D_MODEL), F32),
            pltpu.VMEM((T + 8, 2 * D_MODEL), F32),
            pltpu.VMEM((T, 2 * D_MODEL), F32),
            pltpu.VMEM((1, LANES), F32),
            pltpu.VMEM((SSD_CONV - 1, T, LANES), F32),
        ],
        compiler_params=pltpu.CompilerParams(
            dimension_semantics=("arbitrary", "arbitrary"), vmem_limit_bytes=VMEM_LIMIT),
        name="ssd",
    )(proj, proj, sm, smt, convw, convb, prow, pcol, dskip_e, norm, e64)


def _fox_kernel(q_ref, k_ref, vt_ref, cq_ref, c_ref, o_ref, qh_scr, m_scr, acc_scr, *,
                tq, tk, c_lane0):
    pair = pl.program_id(1)
    i = pl.program_id(2)
    nstrip = tq // STRIP
    lane_k = lax.broadcasted_iota(jnp.int32, (tk, LANES), 1)
    lane = lax.broadcasted_iota(jnp.int32, (tq, LANES), 1)
    q = q_ref[...]
    zero = jnp.zeros_like(q)
    qh_scr[0] = jnp.where(lane < HEADDIM, q, zero)
    qh_scr[1] = jnp.where(lane >= HEADDIM, q, zero)
    m_scr[...] = jnp.full_like(m_scr, NEG_BIG)
    acc_scr[...] = jnp.zeros_like(acc_scr)
    ones = jnp.ones((ONES_ROWS, tk), BF16)

    def scores(hh, st, nkv, k):
        return _dot_nt(k[0:nkv], qh_scr[hh, st * STRIP:(st + 1) * STRIP, :])

    def softmax_pv(hh, st, nkv, s, ck2, vta, koff, partial):
        cols = slice(st * STRIP, (st + 1) * STRIP)
        t = s - ck2[0:nkv]
        vta = vta[:, 0:nkv]
        if partial:
            kv_pos = koff + lax.broadcasted_iota(jnp.int32, (nkv, STRIP), 0)
            q_pos = i * tq + st * STRIP + lax.broadcasted_iota(jnp.int32, (nkv, STRIP), 1)
            t = jnp.where(kv_pos <= q_pos, t, NEG_BIG)
        cq2 = cq_ref[hh:hh + 1, cols] * LOG2E
        m_prev = m_scr[hh, :, cols]
        m_new = jnp.maximum(m_prev, jnp.max(t, axis=0, keepdims=True) + cq2)
        alpha = jnp.exp2(m_prev - m_new)
        p = jnp.exp2((t - (m_new - cq2)).astype(BF16))
        acc_scr[hh, :, cols] = acc_scr[hh, :, cols] * alpha + _dot(vta, p)
        m_scr[hh, :, cols] = m_new

    def strips_of(diag):
        items = []
        for hh in range(2):
            for st in range(nstrip):
                nkv, partial = tk, False
                if diag is not None:
                    kv_lo, q_lo = diag * tk, st * STRIP
                    if kv_lo > q_lo + STRIP - 1:
                        continue
                    nkv = min(tk, q_lo + STRIP - kv_lo)
                    partial = kv_lo + nkv - 1 > q_lo
                items.append((hh, st, nkv, partial))
        return items

    def blocks(j0, diags):
        koffs, ks, ck2s, vtas, items = [], [], [], [], []
        for b, diag in enumerate(diags):
            koff = pl.multiple_of((j0 + b) * tk, tk)
            cblk = c_ref[pl.ds(koff, tk), :]
            ck2, vta = [], []
            for hh in range(2):
                sel = lane_k == (c_lane0 + 2 * pair + hh)
                ck2.append(jnp.sum(jnp.where(sel, cblk, 0.0), axis=1, keepdims=True) * LOG2E)
                vta.append(jnp.concatenate(
                    [vt_ref[hh * HEADDIM:(hh + 1) * HEADDIM, pl.ds(koff, tk)], ones], axis=0))
            koffs.append(koff)
            ks.append(k_ref[pl.ds(koff, tk), :])
            ck2s.append(ck2)
            vtas.append(vta)
            items += [(b,) + it for it in strips_of(diag)]
        pending = [scores(hh, st, nkv, ks[b]) for b, hh, st, nkv, _ in items[:LOOKAHEAD]]
        for n, (b, hh, st, nkv, partial) in enumerate(items):
            if n + LOOKAHEAD < len(items):
                nb, nhh, nst, nnkv, _ = items[n + LOOKAHEAD]
                pending.append(scores(nhh, nst, nnkv, ks[nb]))
            softmax_pv(hh, st, nkv, pending.pop(0), ck2s[b][hh], vtas[b][hh], koffs[b], partial)

    ndiag = tq // tk
    nfull = i * ndiag
    unroll = math.gcd(ndiag, FOX_UNROLL)

    def body(j, carry):
        blocks(j * unroll, [None] * unroll)
        return carry

    lax.fori_loop(0, nfull // unroll, body, 0)
    blocks(nfull, list(range(ndiag)))

    outs = []
    for hh in range(2):
        acc = acc_scr[hh]
        outs.append(acc[0:HEADDIM, :] * (1.0 / acc[HEADDIM:HEADDIM + 1, :]))
    o_ref[...] = jnp.concatenate(outs, axis=0).T.astype(BF16)


def _fox(proj, vt, c8, c, *, bsz, seq, tq, tk, q_cb0, k_cb0, c_lane0):
    m = bsz * seq
    nq = seq // tq
    npair = FOX_HEADS // 2
    kern = functools.partial(_fox_kernel, tq=tq, tk=tk, c_lane0=c_lane0)
    return pl.pallas_call(
        kern,
        grid=(bsz, npair, nq),
        in_specs=[
            pl.BlockSpec((None, tq, LANES), lambda b, p, i: (q_cb0 + p, b * nq + i, 0)),
            pl.BlockSpec((None, seq, LANES), lambda b, p, i: (k_cb0 + p, b, 0)),
            pl.BlockSpec((None, LANES, seq), lambda b, p, i: (p, 0, b)),
            pl.BlockSpec((None, 8, tq), lambda b, p, i: (p, 0, b * nq + i)),
            pl.BlockSpec((seq, LANES), lambda b, p, i: (b, 0)),
        ],
        out_specs=pl.BlockSpec((None, tq, LANES), lambda b, p, i: (p, b * nq + i, 0)),
        out_shape=jax.ShapeDtypeStruct((npair, m, LANES), BF16),
        scratch_shapes=[
            pltpu.VMEM((2, tq, LANES), BF16),
            pltpu.VMEM((2, 1, tq), F32),
            pltpu.VMEM((2, HEADDIM + ONES_ROWS, tq), F32),
        ],
        compiler_params=pltpu.CompilerParams(
            dimension_semantics=("arbitrary", "arbitrary", "arbitrary"),
            vmem_limit_bytes=VMEM_LIMIT),
        name="fox",
    )(proj, proj, vt, c8, c)


def _outproj0_kernel(y_ref, o_ref, z_ref, x_ref, w_ref, g_ref, out_ref):
    y = _cat_blocks(y_ref, 0, CB)
    o = _cat_blocks(o_ref, 0, CB).astype(F32)
    z = _cat_blocks(z_ref, 0, CB).astype(F32)
    og = (o * _silu(z)).astype(BF16)
    acc = _dot(y, w_ref[0:D_MODEL, :]) + _dot(og, w_ref[D_MODEL:2 * D_MODEL, :])
    ms = jnp.mean(acc * acc, axis=-1, keepdims=True)
    out_ref[...] = x_ref[...] + acc * lax.rsqrt(ms + EPS) * g_ref[...]


def _outproj0(y, o, proj, x2, w, g, *, tm):
    m = x2.shape[0]
    return pl.pallas_call(
        _outproj0_kernel,
        grid=(m // tm,),
        in_specs=[
            pl.BlockSpec((CB, tm, LANES), lambda i: (0, i, 0)),
            pl.BlockSpec((CB, tm, LANES), lambda i: (0, i, 0)),
            pl.BlockSpec((CB, tm, LANES), lambda i: (1, i, 0)),
            pl.BlockSpec((tm, D_MODEL), lambda i: (i, 0)),
            pl.BlockSpec(w.shape, lambda i: (0, 0)),
            pl.BlockSpec((1, D_MODEL), lambda i: (0, 0)),
        ],
        out_specs=pl.BlockSpec((tm, D_MODEL), lambda i: (i, 0)),
        out_shape=jax.ShapeDtypeStruct((m, D_MODEL), F32),
        compiler_params=pltpu.CompilerParams(
            dimension_semantics=("arbitrary",), vmem_limit_bytes=VMEM_LIMIT),
        name="outproj0",
    )(y, o, proj, x2, w, g)


def _inproj1_kernel(x_ref, g_ref, w_ref, h_ref, z_ref, xn_scr, *, seg):
    j = pl.program_id(1)

    @pl.when(j == 0)
    def _():
        x = x_ref[...]
        ms = jnp.mean(x * x, axis=-1, keepdims=True)
        xn_scr[...] = (x * lax.rsqrt(ms + EPS) * g_ref[...]).astype(BF16)

    acc = _dot(xn_scr[...], w_ref[...])
    hv = acc[:, 0:seg] * _sigmoid(acc[:, seg:2 * seg])
    for c in range(seg // LANES):
        h_ref[c] = hv[:, c * LANES:(c + 1) * LANES].astype(BF16)
        z_ref[c] = acc[:, 2 * seg + c * LANES:2 * seg + (c + 1) * LANES].astype(BF16)


def _inproj1(x2, g, w, *, tm, seg):
    m = x2.shape[0]
    tn = 3 * seg
    ncol = w.shape[1] // tn
    scb = seg // LANES
    kern = functools.partial(_inproj1_kernel, seg=seg)
    return pl.pallas_call(
        kern,
        grid=(m // tm, ncol),
        in_specs=[
            pl.BlockSpec((tm, D_MODEL), lambda i, j: (i, 0)),
            pl.BlockSpec((1, D_MODEL), lambda i, j: (0, 0)),
            pl.BlockSpec((D_MODEL, tn), lambda i, j: (0, j)),
        ],
        out_specs=[
            pl.BlockSpec((scb, tm, LANES), lambda i, j: (j, i, 0)),
            pl.BlockSpec((scb, tm, LANES), lambda i, j: (j, i, 0)),
        ],
        out_shape=[
            jax.ShapeDtypeStruct((D_CONV // LANES, m, LANES), BF16),
            jax.ShapeDtypeStruct((D_CONV // LANES, m, LANES), BF16),
        ],
        scratch_shapes=[pltpu.VMEM((tm, D_MODEL), BF16)],
        compiler_params=pltpu.CompilerParams(
            dimension_semantics=("arbitrary", "arbitrary"), vmem_limit_bytes=VMEM_LIMIT),
        name="inproj1",
    )(x2, g, w)


def _conv1_kernel(h_ref, z_ref, x_ref, cw_ref, cb_ref, lg_ref, lb_ref, w_ref, g_ref,
                  out_ref, ext_scr, hc_scr, sh_scr, *, T):
    t = pl.program_id(1)
    ncb = D_CONV // LANES
    halo = CONV_HALO

    @pl.when(t == 0)
    def _():
        ext_scr[0:halo, :] = jnp.zeros((halo, D_CONV), F32)

    for c in range(ncb):
        ext_scr[halo:halo + T, c * LANES:(c + 1) * LANES] = h_ref[c].astype(F32)
    span = T + halo - SUBLANES
    for c in range(ncb):
        cs_ = slice(c * LANES, (c + 1) * LANES)
        for r in range(1, SUBLANES):
            sh_scr[r - 1] = ext_scr[r:r + span, cs_]
        acc = cb_ref[:, cs_]
        for k in range(CONV_WIDTH):
            base, r = divmod(halo - (CONV_WIDTH - 1) + k, SUBLANES)
            base *= SUBLANES
            if r == 0:
                src = ext_scr[base:base + T, cs_]
            else:
                src = sh_scr[r - 1, base:base + T, :]
            acc = acc + src * cw_ref[k:k + 1, cs_]
        hc_scr[:, cs_] = acc
    ext_scr[0:halo, :] = ext_scr[T:T + halo, :]

    for r0 in range(0, T, EPI_ROWS):
        rs = slice(r0, r0 + EPI_ROWS)
        hc = hc_scr[rs, :]
        mu = jnp.mean(hc, axis=-1, keepdims=True)
        xc = hc - mu
        var = jnp.mean(xc * xc, axis=-1, keepdims=True)
        hn = xc * lax.rsqrt(var + EPS) * lg_ref[...] + lb_ref[...]
        z = jnp.concatenate([z_ref[c, rs, :] for c in range(ncb)], axis=-1).astype(F32)
        hg = (_silu(hn) * _silu(z)).astype(BF16)
        acc = _dot(hg, w_ref[...])
        ms = jnp.mean(acc * acc, axis=-1, keepdims=True)
        out_ref[rs, :] = x_ref[rs, :] + acc * lax.rsqrt(ms + EPS) * g_ref[...]


def _conv1(h, z, x2, cw, cb, lg, lb, w, g, *, bsz, seq, T):
    m = bsz * seq
    nt = seq // T
    ncb = D_CONV // LANES
    kern = functools.partial(_conv1_kernel, T=T)
    const2 = lambda b, t: (0, 0)
    return pl.pallas_call(
        kern,
        grid=(bsz, nt),
        in_specs=[
            pl.BlockSpec((ncb, T, LANES), lambda b, t: (0, b * nt + t, 0)),
            pl.BlockSpec((ncb, T, LANES), lambda b, t: (0, b * nt + t, 0)),
            pl.BlockSpec((T, D_MODEL), lambda b, t: (b * nt + t, 0)),
            pl.BlockSpec(cw.shape, const2),
            pl.BlockSpec(cb.shape, const2),
            pl.BlockSpec(lg.shape, const2),
            pl.BlockSpec(lb.shape, const2),
            pl.BlockSpec(w.shape, const2),
            pl.BlockSpec(g.shape, const2),
        ],
        out_specs=pl.BlockSpec((T, D_MODEL), lambda b, t: (b * nt + t, 0)),
        out_shape=jax.ShapeDtypeStruct((m, D_MODEL), F32),
        scratch_shapes=[
            pltpu.VMEM((T + CONV_HALO, D_CONV), F32),
            pltpu.VMEM((T, D_CONV), F32),
            pltpu.VMEM((SUBLANES - 1, T + CONV_HALO - SUBLANES, LANES), F32),
        ],
        compiler_params=pltpu.CompilerParams(
            dimension_semantics=("arbitrary", "arbitrary"), vmem_limit_bytes=VMEM_LIMIT),
        name="conv1",
    )(h, z, x2, cw, cb, lg, lb, w, g)


def _pad_lanes(v, width=LANES):
    return jnp.pad(v, (0, width - v.shape[0]))


def _even_layer(x2, bsz, seq, g_pre, w_in, conv_w, conv_b, dt_bias, a_log, d_skip, fgate_b,
                ssd_norm, w_out, g_post):
    d_xbc = D_MODEL + 2 * SSD_GROUPS * D_STATE
    o_z, o_xbc = 0, 2 * D_MODEL
    o_dt = o_xbc + d_xbc
    o_q = o_dt + SSD_HEADS
    o_k = o_q + D_MODEL
    o_v = o_k + D_MODEL
    o_f = o_v + D_MODEL
    q_scale = HEADDIM ** -0.5 * LOG2E
    w_main = jnp.concatenate(
        [w_in[:, o_z:o_dt], w_in[:, o_q:o_k] * q_scale, w_in[:, o_k:o_v]],
        axis=1).astype(BF16)
    wvt = w_in[:, o_v:o_f].T.astype(BF16)
    w_small = jnp.concatenate(
        [w_in[:, o_dt:o_q], w_in[:, o_f:o_f + FOX_HEADS],
         jnp.zeros((D_MODEL, LANES - SSD_HEADS - FOX_HEADS), F32)], axis=1)
    wsh = w_small.astype(BF16)
    wsl = (w_small - wsh.astype(F32)).astype(BF16)

    proj, vt, sm, smt = _inproj0(x2, g_pre[None, :], w_main, wvt, wsh, wsl,
                                 tm=TILES.proj_rows, tn=TILES.proj_cols)

    zeros16 = jnp.zeros((SSD_HEADS,), F32)
    prow = jnp.stack([
        _pad_lanes(dt_bias),
        _pad_lanes(a_log),
        _pad_lanes(jnp.concatenate([zeros16, fgate_b])),
    ] + [jnp.zeros((LANES,), F32)] * 5)
    pcol = jnp.stack([jnp.broadcast_to(dt_bias[:, None], (SSD_HEADS, LANES)),
                      jnp.broadcast_to(a_log[:, None], (SSD_HEADS, LANES))])
    dskip_e = jnp.repeat(d_skip, HEADDIM)[None, :]
    head_of_lane = jnp.arange(D_MODEL) // HEADDIM
    e64 = (jnp.arange(LANES)[:, None] == head_of_lane[None, :]).astype(BF16)

    y, c, c8 = _ssd(proj, sm, smt, conv_w, conv_b[None, :], prow, pcol, dskip_e,
                    ssd_norm[None, :], e64, bsz=bsz, seq=seq, T=TILES.ssd_rows)

    o = _fox(proj, vt, c8, c, bsz=bsz, seq=seq, tq=TILES.fox_q, tk=TILES.fox_kv,
             q_cb0=4 * CB, k_cb0=5 * CB, c_lane0=SSD_HEADS)
    return _outproj0(y, o, proj, x2, w_out.astype(BF16), g_post[None, :], tm=TILES.out_rows)


def _odd_layer(x2, bsz, seq, g_pre, w_in, conv_w, conv_b, ln_g, ln_b, w_out, g_post):
    seg = TILES.glu_seg
    cols = []
    for j in range(D_CONV // seg):
        for part in range(3):
            cols.append(w_in[:, part * D_CONV + j * seg:part * D_CONV + (j + 1) * seg])
    w = jnp.concatenate(cols, axis=1).astype(BF16)
    h, z = _inproj1(x2, g_pre[None, :], w, tm=TILES.proj_rows, seg=seg)
    return _conv1(h, z, x2, conv_w, conv_b[None, :], ln_g[None, :], ln_b[None, :],
                  w_out.astype(BF16), g_post[None, :], bsz=bsz, seq=seq, T=TILES.conv_rows)


def kernel(x, e_norm_pre, e_w_in, e_conv_w, e_conv_b, e_dt_bias, e_a_log, e_d_skip, e_fgate_b,
           e_ssd_norm, e_w_out, e_norm_post, o_norm_pre, o_w_in, o_conv_w, o_conv_b, o_ln_g,
           o_ln_b, o_w_out, o_norm_post):
    bsz, seq, d = x.shape
    x2 = x.reshape(bsz * seq, d)
    depth = e_w_in.shape[0] + o_w_in.shape[0]
    for layer in range(depth):
        i = layer // 2
        if layer % 2 == 0:
            x2 = _even_layer(x2, bsz, seq, e_norm_pre[i], e_w_in[i], e_conv_w[i], e_conv_b[i],
                             e_dt_bias[i], e_a_log[i], e_d_skip[i], e_fgate_b[i],
                             e_ssd_norm[i], e_w_out[i], e_norm_post[i])
        else:
            x2 = _odd_layer(x2, bsz, seq, o_norm_pre[i], o_w_in[i], o_conv_w[i], o_conv_b[i],
                            o_ln_g[i], o_ln_b[i], o_w_out[i], o_norm_post[i])
    return x2.reshape(bsz, seq, d)
```
